```python
import math
import jax
import jax.numpy as jnp
from jax import lax
import numpy as np

D_MODEL = 1024
BATCH = 16
SEQ = 2048
DEPTH = 2

GRID_W = 64
CTX_LEN = 256

D_HY = 512
HY_ORDER = 2
HY_SHORT = 3
HY_BANDS = 16
HY_EMB = 1 + 2 * HY_BANDS
HY_HID = 64
HY_COLS = (HY_ORDER + 1) * D_HY
HY_FILT_COLS = HY_ORDER * 2 * D_HY
HY_DECAY_MIN = math.log(1e2) / 1.5
HY_DECAY_MAX = math.log(1e2) / 0.3

D_CF = 512
CF_KERNEL = 31
CF_COLS = 2 * D_CF

D_S5 = 512
S5_GROUP = 16
S5_GROUPS = D_S5 // S5_GROUP
S5_STATE = 64
S5_MAX_RE = -1e-4

N_BRANCH = 3
CF_OFF = HY_COLS
S5_OFF = CF_OFF + CF_COLS
GATE_OFF = S5_OFF + D_S5
IN_COLS = GATE_OFF + N_BRANCH * D_MODEL

D_FF = 2816
N_EXPERTS = 8
TOP_K = 2
D_EXPERT = 3584
N_DENSE = (DEPTH + 1) // 2
N_MOE = DEPTH // 2

ALPHA = (2.0 * DEPTH) ** 0.25
BETA = (8.0 * DEPTH) ** -0.25
LN_EPS = 1e-5

kernel_name = "hybrid_hyena_conformer_s5_moe_dit"


def layer_norm(x, g=None, b=None):
    xf = x.astype(jnp.float32)
    mu = jnp.mean(xf, axis=-1, keepdims=True)
    var = jnp.mean(jnp.square(xf - mu), axis=-1, keepdims=True)
    y = ((xf - mu) * lax.rsqrt(var + LN_EPS)).astype(x.dtype)
    if g is not None:
        y = y * g + b
    return y


def modulate(x, shift, scale):
    return layer_norm(x) * (1 + scale) + shift


def dwconv(u, w, b):
    pad = (w.shape[0] - 1) // 2
    y = lax.conv_general_dilated(
        u, w[:, None, :].astype(u.dtype), window_strides=(1,), padding=[(pad, pad)],
        dimension_numbers=("NWC", "WIO", "NWC"), feature_group_count=u.shape[-1])
    return y + b


def hyena_filters(L, w1, b1, w2, b2, w3, b3, freq, decay):
    f32 = jnp.float32
    t = jnp.arange(L, dtype=f32)[:, None]
    tn = t / max(L - 1, 1)
    bands = jnp.arange(1, HY_BANDS + 1, dtype=f32)
    ang = t * bands * (2.0 * math.pi / L)
    z = jnp.concatenate([tn, jnp.cos(ang), jnp.sin(ang)], axis=-1)
    freq = freq.astype(f32)
    h = jnp.sin(freq[0] * (z @ w1.astype(f32) + b1.astype(f32)))
    h = jnp.sin(freq[1] * (h @ w2.astype(f32) + b2.astype(f32)))
    h = h @ w3.astype(f32) + b3.astype(f32)
    h = h * jnp.exp(-tn * jnp.abs(decay.astype(f32)))
    return h.reshape(L, HY_ORDER, 2, D_HY)


def two_sided_long_conv(u, h_fwd, h_bwd, bias):
    L = u.shape[1]
    n = 2 * L
    k = jnp.concatenate([h_fwd, jnp.zeros_like(h_fwd[:1]), h_bwd[1:][::-1]], axis=0)
    uf = u.astype(jnp.float32)
    y = jnp.fft.irfft(jnp.fft.rfft(uf, n=n, axis=1) * jnp.fft.rfft(k, n=n, axis=0)[None], n=n, axis=1)[:, :L]
    return (y + uf * bias.astype(jnp.float32)).astype(u.dtype)


def hyena_branch(p, short_w, short_b, f_w1, f_b1, f_w2, f_b2, f_w3, f_b3, freq, decay, bias, w_out):
    L = p.shape[1]
    parts = jnp.split(dwconv(p, short_w, short_b), HY_ORDER + 1, axis=-1)
    filt = hyena_filters(L, f_w1, f_b1, f_w2, f_b2, f_w3, f_b3, freq, decay)
    z = parts[-1]
    for o in range(HY_ORDER):
        z = parts[o] * two_sided_long_conv(z, filt[:, o, 0], filt[:, o, 1], bias[o])
    return z @ w_out


def conformer_branch(p, grid_rows, dw_w, dw_b, ln_g, ln_b, w_out):
    a, g = jnp.split(p, 2, axis=-1)
    u = a * jax.nn.sigmoid(g)
    bsz, L, C = u.shape
    if grid_rows is not None:
        u = dwconv(u.reshape(bsz * grid_rows, GRID_W, C), dw_w, dw_b).reshape(bsz, L, C)
    else:
        u = dwconv(u, dw_w, dw_b)
    u = jax.nn.silu(layer_norm(u, ln_g, ln_b))
    return u @ w_out


def _ssm_combine(left, right):
    a1r, a1i, b1r, b1i = left
    a2r, a2i, b2r, b2i = right
    return (a2r * a1r - a2i * a1i, a2r * a1i + a2i * a1r,
            a2r * b1r - a2i * b1i + b2r, a2r * b1i + a2i * b1r + b2i)


def s5_discretize(a_re, a_im, log_dt, b_re, b_im):
    f32 = jnp.float32
    lam_re = jnp.minimum(a_re.astype(f32), S5_MAX_RE)
    lam_im = a_im.astype(f32)
    dt = jnp.exp(log_dt.astype(f32))[:, None]
    mag = jnp.exp(lam_re * dt)
    ang = lam_im * dt
    abar_re = mag * jnp.cos(ang)
    abar_im = mag * jnp.sin(ang)
    den = lam_re * lam_re + lam_im * lam_im
    q_re = ((abar_re - 1.0) * lam_re + abar_im * lam_im) / den
    q_im = (abar_im * lam_re - (abar_re - 1.0) * lam_im) / den
    br = b_re.astype(f32)
    bi = b_im.astype(f32)
    bb_re = q_re[..., None] * br - q_im[..., None] * bi
    bb_im = q_re[..., None] * bi + q_im[..., None] * br
    return abar_re, abar_im, bb_re, bb_im


def s5_scans(u, a_re, a_im, log_dt, b_re, b_im, c_re, c_im, init, want_final):
    f32 = jnp.float32
    bsz, L, _ = u.shape
    ug = u.astype(f32).reshape(bsz, L, S5_GROUPS, S5_GROUP)
    ys, finals = [], []
    for d, reverse in enumerate((False, True)):
        abr, abi, bbr, bbi = s5_discretize(a_re[d], a_im[d], log_dt[d], b_re[d], b_im[d])
        bur = jnp.einsum("blgc,gpc->blgp", ug, bbr)
        bui = jnp.einsum("blgc,gpc->blgp", ug, bbi)
        if init is not None:
            s0r, s0i = init[d]
            first = L - 1 if reverse else 0
            bur = bur.at[:, first].add(abr * s0r - abi * s0i)
            bui = bui.at[:, first].add(abr * s0i + abi * s0r)
        ar = jnp.broadcast_to(abr, (1, L) + abr.shape)
        ai = jnp.broadcast_to(abi, (1, L) + abi.shape)
        _, _, sr, si = lax.associative_scan(_ssm_combine, (ar, ai, bur, bui), reverse=reverse, axis=1)
        ys.append(jnp.einsum("blgp,gcp->blgc", sr, c_re[d].astype(f32))
                  - jnp.einsum("blgp,gcp->blgc", si, c_im[d].astype(f32)))
        if want_final:
            last = 0 if reverse else L - 1
            finals.append((sr[:, last], si[:, last]))
    y = (ys[0] + ys[1]).reshape(bsz, L, D_S5)
    return y, finals


def s5_branch(p, s5_p, d_skip, w_glu, b_glu, w_out, init, want_final):
    y, finals = s5_scans(p, *s5_p, init, want_final)
    y = y.astype(p.dtype) + d_skip * p
    g = jax.nn.gelu(y)
    y = g * jax.nn.sigmoid(g @ w_glu + b_glu)
    return y @ w_out, finals


def token_mixer(p, grid_rows, hy_p, cf_p, s5_p, s5_out_p, w_o, init, want_final):
    y_hy = hyena_branch(p[..., :CF_OFF], *hy_p)
    y_cf = conformer_branch(p[..., CF_OFF:S5_OFF], grid_rows, *cf_p)
    y_s5, finals = s5_branch(p[..., S5_OFF:GATE_OFF], s5_p, *s5_out_p, init, want_final)
    g_hy, g_cf, g_s5 = jnp.split(jax.nn.sigmoid(p[..., GATE_OFF:]), N_BRANCH, axis=-1)
    return (g_hy * y_hy + g_cf * y_cf + g_s5 * y_s5) @ w_o, finals


def swiglu(h, w1, w3, w2):
    return (jax.nn.silu(h @ w1) * (h @ w3)) @ w2


def moe_swiglu(h, router, w1, w3, w2):
    logits = (h @ router).astype(jnp.float32)
    top_v, top_i = lax.top_k(logits, TOP_K)
    top_w = jax.nn.softmax(top_v, axis=-1)
    comb = jnp.einsum("blk,blke->ble", top_w,
                      jax.nn.one_hot(top_i, N_EXPERTS, dtype=jnp.float32)).astype(h.dtype)
    out = comb[..., 0:1] * swiglu(h, w1[0], w3[0], w2[0])
    for e in range(1, N_EXPERTS):
        out = out + comb[..., e:e + 1] * swiglu(h, w1[e], w3[e], w2[e])
    return out


def channel_mixer(h, layer, ffn_w1, ffn_w3, ffn_w2, moe_router, moe_w1, moe_w3, moe_w2):
    i = layer // 2
    if layer % 2 == 0:
        return swiglu(h, ffn_w1[i], ffn_w3[i], ffn_w2[i])
    return moe_swiglu(h, moe_router[i], moe_w1[i], moe_w3[i], moe_w2[i])


def setup_inputs(seed: int = 0) -> dict:
    key = jax.random.key(seed)
    keys = iter(jax.random.split(key, 64))
    f32 = jnp.float32

    def nrm(shape, scale):
        return jax.random.normal(next(keys), shape, f32) * scale

    def unif(shape, lo, hi):
        return jax.random.uniform(next(keys), shape, f32, lo, hi)

    D = D_MODEL
    G, P, K = S5_GROUPS, S5_STATE, S5_GROUP
    n_idx = jnp.arange(S5_STATE, dtype=f32)
    return {
        "x": nrm((BATCH, SEQ, D), 1.0),
        "c": nrm((BATCH, D), 1.0),
        "ctx": nrm((BATCH, CTX_LEN, D), 1.0),
        "c_ctx": nrm((D,), 1.0),
        "w_mod": nrm((DEPTH, D, 6 * D), 0.5 * D ** -0.5),
        "b_mod": nrm((DEPTH, 6 * D), 0.02),
        "w_in": nrm((DEPTH, D, IN_COLS), D ** -0.5),
        "hy_short_w": nrm((DEPTH, HY_SHORT, HY_COLS), HY_SHORT ** -0.5),
        "hy_short_b": nrm((DEPTH, HY_COLS), 0.02),
        "hy_f_w1": nrm((DEPTH, HY_EMB, HY_HID), HY_EMB ** -0.5),
        "hy_f_b1": nrm((DEPTH, HY_HID), 0.1),
        "hy_f_w2": nrm((DEPTH, HY_HID, HY_HID), HY_HID ** -0.5),
        "hy_f_b2": nrm((DEPTH, HY_HID), 0.1),
        "hy_f_w3": nrm((DEPTH, HY_HID, HY_FILT_COLS), 0.05 * HY_HID ** -0.5),
        "hy_f_b3": nrm((DEPTH, HY_FILT_COLS), 0.01),
        "hy_freq": 1.0 + nrm((DEPTH, 2, HY_HID), 0.01),
        "hy_decay": unif((DEPTH, HY_FILT_COLS), HY_DECAY_MIN, HY_DECAY_MAX),
        "hy_bias": nrm((DEPTH, HY_ORDER, D_HY), 0.5),
        "w_hy_out": nrm((DEPTH, D_HY, D), D_HY ** -0.5),
        "cf_dw_w": nrm((DEPTH, CF_KERNEL, D_CF), CF_KERNEL ** -0.5),
        "cf_dw_b": nrm((DEPTH, D_CF), 0.02),
        "cf_ln_g": 1.0 + nrm((DEPTH, D_CF), 0.02),
        "cf_ln_b": nrm((DEPTH, D_CF), 0.02),
        "w_cf_out": nrm((DEPTH, D_CF, D), D_CF ** -0.5),
        "s5_a_re": -0.5 + nrm((DEPTH, 2, G, P), 0.01),
        "s5_a_im": math.pi * n_idx + nrm((DEPTH, 2, G, P), 0.01),
        "s5_log_dt": unif((DEPTH, 2, G), math.log(1e-3), math.log(1e-1)),
        "s5_b_re": nrm((DEPTH, 2, G, P, K), K ** -0.5),
        "s5_b_im": nrm((DEPTH, 2, G, P, K), K ** -0.5),
        "s5_c_re": nrm((DEPTH, 2, G, K, P), P ** -0.5),
        "s5_c_im": nrm((DEPTH, 2, G, K, P), P ** -0.5),
        "s5_d": nrm((DEPTH, D_S5), 1.0),
        "s5_w_glu": nrm((DEPTH, D_S5, D_S5), D_S5 ** -0.5),
        "s5_b_glu": nrm((DEPTH, D_S5), 0.02),
        "w_s5_out": nrm((DEPTH, D_S5, D), D_S5 ** -0.5),
        "w_o": nrm((DEPTH, D, D), BETA * D ** -0.5),
        "ln1_g": 1.0 + nrm((DEPTH, D), 0.02),
        "ln1_b": nrm((DEPTH, D), 0.02),
        "ln2_g": 1.0 + nrm((DEPTH, D), 0.02),
        "ln2_b": nrm((DEPTH, D), 0.02),
        "ffn_w1": nrm((N_DENSE, D, D_FF), D ** -0.5),
        "ffn_w3": nrm((N_DENSE, D, D_FF), D ** -0.5),
        "ffn_w2": nrm((N_DENSE, D_FF, D), BETA * D_FF ** -0.5),
        "moe_router": nrm((N_MOE, D, N_EXPERTS), D ** -0.5),
        "moe_w1": nrm((N_MOE, N_EXPERTS, D, D_EXPERT), D ** -0.5),
        "moe_w3": nrm((N_MOE, N_EXPERTS, D, D_EXPERT), D ** -0.5),
        "moe_w2": nrm((N_MOE, N_EXPERTS, D_EXPERT, D), BETA * D_EXPERT ** -0.5),
    }


def reference(x, c, ctx, c_ctx, w_mod, b_mod, w_in,
              hy_short_w, hy_short_b, hy_f_w1, hy_f_b1, hy_f_w2, hy_f_b2, hy_f_w3, hy_f_b3,
              hy_freq, hy_decay, hy_bias, w_hy_out,
              cf_dw_w, cf_dw_b, cf_ln_g, cf_ln_b, w_cf_out,
              s5_a_re, s5_a_im, s5_log_dt, s5_b_re, s5_b_im, s5_c_re, s5_c_im,
              s5_d, s5_w_glu, s5_b_glu, w_s5_out,
              w_o, ln1_g, ln1_b, ln2_g, ln2_b,
              ffn_w1, ffn_w3, ffn_w2, moe_router, moe_w1, moe_w3, moe_w2):
    rows = x.shape[1] // GRID_W
    xc = ctx
    silu_c = jax.nn.silu(c)
    silu_cc = jax.nn.silu(c_ctx)
    ffn_p = (ffn_w1, ffn_w3, ffn_w2, moe_router, moe_w1, moe_w3, moe_w2)
    for l in range(DEPTH):
        last = l == DEPTH - 1
        sh1, sc1, g1, sh2, sc2, g2 = [m[:, None, :] for m in
                                      jnp.split(silu_c @ w_mod[l] + b_mod[l], 6, axis=-1)]
        csh1, csc1, cg1, csh2, csc2, cg2 = jnp.split(silu_cc @ w_mod[l] + b_mod[l], 6, axis=-1)
        hy_p = (hy_short_w[l], hy_short_b[l], hy_f_w1[l], hy_f_b1[l], hy_f_w2[l], hy_f_b2[l],
                hy_f_w3[l], hy_f_b3[l], hy_freq[l], hy_decay[l], hy_bias[l], w_hy_out[l])
        cf_p = (cf_dw_w[l], cf_dw_b[l], cf_ln_g[l], cf_ln_b[l], w_cf_out[l])
        s5_p = (s5_a_re[l], s5_a_im[l], s5_log_dt[l], s5_b_re[l], s5_b_im[l], s5_c_re[l], s5_c_im[l])
        s5_out_p = (s5_d[l], s5_w_glu[l], s5_b_glu[l], w_s5_out[l])

        hc = modulate(xc, csh1, csc1)
        if last:
            _, finals = s5_scans(hc @ w_in[l][:, S5_OFF:GATE_OFF], *s5_p, None, True)
        else:
            yc, finals = token_mixer(hc @ w_in[l], None, hy_p, cf_p, s5_p, s5_out_p, w_o[l], None, True)
            xc = layer_norm(ALPHA * xc + cg1 * yc, ln1_g[l], ln1_b[l])
            fc = channel_mixer(modulate(xc, csh2, csc2), l, *ffn_p)
            xc = layer_norm(ALPHA * xc + cg2 * fc, ln2_g[l], ln2_b[l])

        hx = modulate(x, sh1, sc1)
        yx, _ = token_mixer(hx @ w_in[l], rows, hy_p, cf_p, s5_p, s5_out_p, w_o[l], finals, False)
        x = layer_norm(ALPHA * x + g1 * yx, ln1_g[l], ln1_b[l])
        fx = channel_mixer(modulate(x, sh2, sc2), l, *ffn_p)
        x = layer_norm(ALPHA * x + g2 * fx, ln2_g[l], ln2_b[l])
    return x
```

```python
import functools
import math

import jax
import jax.numpy as jnp
from jax import lax
from jax.experimental import pallas as pl
from jax.experimental.pallas import tpu as pltpu

F32 = jnp.float32
BF16 = jnp.bfloat16
HIGHEST = lax.Precision.HIGHEST

V7X_VMEM_LIMIT_BYTES = 56 * 1024 * 1024

LN_EPS = 1e-5
D_BRANCH = 512
HY_ORDER = 2
HY_SHORT = 3
HY_ROW_BLOCK = 512
CF_KERNEL = 31
CF_PAD = 16
GRID_W = 64
S5_GROUP = 16
S5_STATE = 64
S5_MAX_RE = -1e-4
S5_BLOCK_CH = 128
S5_BLOCK_ST = S5_BLOCK_CH // S5_GROUP * S5_STATE
N_EXPERTS = 8
ROUTER_LANES = 128


def _cparams(*sem):
    return pltpu.CompilerParams(dimension_semantics=sem, vmem_limit_bytes=V7X_VMEM_LIMIT_BYTES)


def _const_spec(shape):
    nd = len(shape)
    return pl.BlockSpec(shape, lambda *_: (0,) * nd, pipeline_mode=pl.Buffered(1))


def _bdot(a, b):
    return jnp.dot(a, b, preferred_element_type=F32)


def _sigmoid(x):
    return 1.0 / (1.0 + jnp.exp(-x))


def _silu(x):
    return x * _sigmoid(x)


def _gelu_tanh(x):
    return 0.5 * x * (1.0 + jnp.tanh(math.sqrt(2.0 / math.pi) * (x + 0.044715 * (x * x * x))))


def _ln(x):
    mu = jnp.mean(x, axis=-1, keepdims=True)
    xc = x - mu
    var = jnp.mean(xc * xc, axis=-1, keepdims=True)
    return xc * lax.rsqrt(var + LN_EPS)


def _ln_mod(x, shift, scale):
    return _ln(x) * (1.0 + scale) + shift


def _mod_kernel(c_ref, w_ref, b_ref, o_ref):
    s = _silu(c_ref[...])
    o_ref[...] = _bdot(s.astype(BF16), w_ref[...]) + b_ref[...]


def mod_vectors(cc, w, b, *, tn=1536):
    r, d = cc.shape
    n = w.shape[1]
    return pl.pallas_call(
        _mod_kernel,
        grid=(n // tn,),
        in_specs=[pl.BlockSpec((r, d), lambda j: (0, 0)),
                  pl.BlockSpec((d, tn), lambda j: (0, j)),
                  pl.BlockSpec((1, tn), lambda j: (0, j))],
        out_specs=pl.BlockSpec((r, tn), lambda j: (0, j)),
        out_shape=jax.ShapeDtypeStruct((r, n), F32),
        compiler_params=_cparams("arbitrary"),
        name="mod_vectors",
    )(cc, w, b)


def _proj_kernel(x_ref, sh_ref, sc_ref, *refs, glu):
    if glu:
        wa_ref, wg_ref, o_ref, h_ref = refs
    else:
        wa_ref, o_ref, h_ref = refs

    @pl.when(pl.program_id(1) == 0)
    def _():
        h_ref[...] = _ln_mod(x_ref[...], sh_ref[0], sc_ref[0]).astype(BF16)

    y = _bdot(h_ref[...], wa_ref[...])
    if glu:
        y = y * _sigmoid(_bdot(h_ref[...], wg_ref[...]))
    o_ref[...] = y


def ln_mod_proj(x2, shift, scale, ws, *, seq, tn, time_major=False):
    t_tok, d = x2.shape
    bsz = t_tok // seq
    n = ws[0].shape[1]
    tm = min(seq, 1024)
    nt = seq // tm
    nj = n // tn
    glu = len(ws) == 2
    mod_spec = pl.BlockSpec((1, 1, d), lambda i, j: (i // nt, 0, 0))
    w_spec = pl.BlockSpec((d, tn), lambda i, j: (0, j))
    if time_major:
        out_shape = jax.ShapeDtypeStruct((seq, bsz * n), F32)
        out_spec = pl.BlockSpec((tm, tn), lambda i, j: (i % nt, (i // nt) * nj + j))
    else:
        out_shape = jax.ShapeDtypeStruct((t_tok, n), F32)
        out_spec = pl.BlockSpec((tm, tn), lambda i, j: (i, j))
    out = pl.pallas_call(
        functools.partial(_proj_kernel, glu=glu),
        grid=(t_tok // tm, nj),
        in_specs=[pl.BlockSpec((tm, d), lambda i, j: (i, 0)), mod_spec, mod_spec] + [w_spec] * len(ws),
        out_specs=out_spec,
        out_shape=out_shape,
        scratch_shapes=[pltpu.VMEM((tm, d), BF16)],
        compiler_params=_cparams("arbitrary", "arbitrary"),
        name="ln_mod_proj",
    )(x2, shift, scale, *ws)
    if time_major:
        out = out.reshape(seq * bsz, n)
    return out


def _filter_kernel(z_ref, w1_ref, b1_ref, w2_ref, b2_ref, w3_ref, b3_ref, fr_ref, dec_ref, o_ref):
    z = z_ref[...]
    h = jnp.sin(fr_ref[0:1, :] * (jnp.dot(z, w1_ref[...], precision=HIGHEST, preferred_element_type=F32)
                                  + b1_ref[...]))
    h = jnp.sin(fr_ref[1:2, :] * (jnp.dot(h, w2_ref[...], precision=HIGHEST, preferred_element_type=F32)
                                  + b2_ref[...]))
    h = jnp.dot(h, w3_ref[...], precision=HIGHEST, preferred_element_type=F32) + b3_ref[...]
    tn = z[:, 0:1]
    o_ref[...] = h * jnp.exp(-tn * jnp.abs(dec_ref[...]))


def hyena_filters(z, w1, b1, w2, b2, w3, b3, freq, decay, *, tn=512):
    seq, zf = z.shape
    hid = w2.shape[0]
    n = w3.shape[1]
    full = lambda shape: pl.BlockSpec(shape, lambda j: (0, 0))
    return pl.pallas_call(
        _filter_kernel,
        grid=(n // tn,),
        in_specs=[full((seq, zf)), full((zf, hid)), full((1, hid)), full((hid, hid)), full((1, hid)),
                  pl.BlockSpec((hid, tn), lambda j: (0, j)), pl.BlockSpec((1, tn), lambda j: (0, j)),
                  full((2, hid)), pl.BlockSpec((1, tn), lambda j: (0, j))],
        out_specs=pl.BlockSpec((seq, tn), lambda j: (0, j)),
        out_shape=jax.ShapeDtypeStruct((seq, n), F32),
        compiler_params=_cparams("arbitrary"),
        name="hyena_filters",
    )(z, w1, b1, w2, b2, w3, b3, freq, decay)


def _spectrum_kernel(cs_ref, ss_ref, filt_ref, kr_ref, ki_ref, kn_ref, *, tf, n_fft):
    ch = kr_ref.shape[-1]
    seq = filt_ref.shape[0]
    hf = filt_ref[:, 0:ch]
    tau = lax.broadcasted_iota(jnp.int32, (seq, 1), 0)
    hb = jnp.where(tau == 0, 0.0, filt_ref[:, ch:2 * ch])
    s_cos = hf + hb
    s_sin = hb - hf
    f = pl.program_id(1) * tf + lax.broadcasted_iota(jnp.int32, (tf, 1), 0)
    wgt = jnp.where(f == 0, 1.0, 2.0) / n_fft
    kr_ref[0] = wgt * jnp.dot(cs_ref[...], s_cos, precision=HIGHEST, preferred_element_type=F32)
    ki_ref[0] = wgt * jnp.dot(ss_ref[...], s_sin, precision=HIGHEST, preferred_element_type=F32)
    sign = (1 - 2 * (tau & 1)).astype(F32)
    kn_ref[0] = jnp.sum(sign * s_cos, axis=0, keepdims=True) / n_fft


def hyena_spectra(cs, ss, filt, *, tf=256):
    seq = cs.shape[0]
    ch = D_BRANCH
    tf = min(tf, seq)
    return pl.pallas_call(
        functools.partial(_spectrum_kernel, tf=tf, n_fft=2 * seq),
        grid=(HY_ORDER, seq // tf),
        in_specs=[pl.BlockSpec((tf, seq), lambda o, f: (f, 0)),
                  pl.BlockSpec((tf, seq), lambda o, f: (f, 0)),
                  pl.BlockSpec((seq, 2 * ch), lambda o, f: (0, o))],
        out_specs=[pl.BlockSpec((1, tf, ch), lambda o, f: (o, f, 0)),
                   pl.BlockSpec((1, tf, ch), lambda o, f: (o, f, 0)),
                   pl.BlockSpec((1, 1, ch), lambda o, f: (o, 0, 0))],
        out_shape=[jax.ShapeDtypeStruct((HY_ORDER, seq, ch), F32),
                   jax.ShapeDtypeStruct((HY_ORDER, seq, ch), F32),
                   jax.ShapeDtypeStruct((HY_ORDER, 1, ch), F32)],
        compiler_params=_cparams("arbitrary", "arbitrary"),
        name="hyena_spectra",
    )(cs, ss, filt)


def _hyena_kernel(x1_ref, x2_ref, v_ref, sw_ref, sb_ref, cs_ref, ss_ref, kr_ref, ki_ref, kn_ref,
                  bias_ref, z_ref, zb_ref, zr_ref, zi_ref, xs_ref):
    seq = v_ref.shape[0]
    row = lax.broadcasted_iota(jnp.int32, (seq, 1), 0)
    sign = (1 - 2 * (row & 1)).astype(F32)

    def short_conv(p_ref, part):
        p = p_ref[...]
        prev = jnp.where(row == 0, 0.0, pltpu.roll(p, 1, 0))
        nxt = jnp.where(row == seq - 1, 0.0, pltpu.roll(p, seq - 1, 0))
        w = sw_ref[part * HY_SHORT:(part + 1) * HY_SHORT, :]
        return w[0:1] * prev + w[1:2] * p + w[2:3] * nxt + sb_ref[part:part + 1, :]

    rb = min(seq, HY_ROW_BLOCK)
    blocks = [slice(r * rb, (r + 1) * rb) for r in range(seq // rb)]
    z_ref[...] = short_conv(v_ref, 2)
    for o, x_ref in enumerate((x1_ref, x2_ref)):
        z = z_ref[...]
        zb_ref[...] = z.astype(BF16)
        nyq = jnp.sum(sign * z, axis=0, keepdims=True) * kn_ref[o]
        xs_ref[...] = short_conv(x_ref, o)
        for rows in blocks:
            a = _bdot(cs_ref[rows, :], zb_ref[...])
            b = _bdot(ss_ref[rows, :], zb_ref[...])
            kr = kr_ref[o, rows, :]
            ki = ki_ref[o, rows, :]
            zr_ref[rows, :] = (a * kr + b * ki).astype(BF16)
            zi_ref[rows, :] = (a * ki - b * kr).astype(BF16)
        for rows in blocks:
            y = _bdot(cs_ref[rows, :], zr_ref[...]) - _bdot(ss_ref[rows, :], zi_ref[...])
            y = y + sign[rows] * nyq + bias_ref[o:o + 1, :] * z_ref[rows, :]
            z_ref[rows, :] = xs_ref[rows, :] * y


def hyena_mix(p_hy, sw, sb, cs, ss, kr, ki, kn, bias, *, seq, cb=256):
    t_tok = p_hy.shape[0]
    ch = D_BRANCH
    bsz = t_tok // seq
    ncb = ch // cb
    col = lambda part: pl.BlockSpec((seq, cb), lambda j, b: (b, part * ncb + j))
    per_c = lambda rows: pl.BlockSpec((rows, cb), lambda j, b: (0, j))
    spec3 = lambda rows: pl.BlockSpec((HY_ORDER, rows, cb), lambda j, b: (0, 0, j),
                                      pipeline_mode=pl.Buffered(1))
    return pl.pallas_call(
        _hyena_kernel,
        grid=(ncb, bsz),
        in_specs=[col(0), col(1), col(2), per_c(3 * HY_SHORT), per_c(3),
                  _const_spec((seq, seq)), _const_spec((seq, seq)),
                  spec3(seq), spec3(seq), spec3(1), per_c(HY_ORDER)],
        out_specs=pl.BlockSpec((seq, cb), lambda j, b: (b, j)),
        out_shape=jax.ShapeDtypeStruct((t_tok, ch), F32),
        scratch_shapes=[pltpu.VMEM((seq, cb), BF16), pltpu.VMEM((seq, cb), BF16),
                        pltpu.VMEM((seq, cb), BF16), pltpu.VMEM((seq, cb), F32)],
        compiler_params=_cparams("arbitrary", "arbitrary"),
        name="hyena_mix",
    )(p_hy, p_hy, p_hy, sw, sb, cs, ss, kr, ki, kn, bias)


def _conformer_kernel(u_ref, w_ref, b_ref, g_ref, be_ref, o_ref, pad_ref, *, seg, nseg):
    ch = u_ref.shape[-1]
    zeros = jnp.zeros((nseg, CF_PAD, ch), F32)
    pad_ref[:, 0:CF_PAD, :] = zeros
    pad_ref[:, CF_PAD + seg:, :] = zeros
    pad_ref[:, CF_PAD:CF_PAD + seg, :] = u_ref[...].reshape(nseg, seg, ch)
    half = (CF_KERNEL - 1) // 2
    acc = jnp.zeros((nseg, seg, ch), F32)
    for k in range(CF_KERNEL):
        start = CF_PAD - half + k
        acc = acc + w_ref[k:k + 1, :] * pad_ref[:, start:start + seg, :]
    y = _ln(acc + b_ref[...]) * g_ref[...] + be_ref[...]
    o_ref[...] = _silu(y).reshape(nseg * seg, ch)


def conformer_mix(u, dw_w, dw_b, ln_g, ln_b, *, seg):
    t_tok, ch = u.shape
    tm = max(seg, 512)
    nseg = tm // seg
    vec = lambda rows: pl.BlockSpec((rows, ch), lambda i: (0, 0))
    return pl.pallas_call(
        functools.partial(_conformer_kernel, seg=seg, nseg=nseg),
        grid=(t_tok // tm,),
        in_specs=[pl.BlockSpec((tm, ch), lambda i: (i, 0)), vec(CF_KERNEL), vec(1), vec(1), vec(1)],
        out_specs=pl.BlockSpec((tm, ch), lambda i: (i, 0)),
        out_shape=jax.ShapeDtypeStruct((t_tok, ch), F32),
        scratch_shapes=[pltpu.VMEM((nseg, seg + 2 * CF_PAD, ch), F32)],
        compiler_params=_cparams("arbitrary"),
        name="conformer_mix",
    )(u, dw_w, dw_b, ln_g, ln_b)


def _s5_kernel(u_ref, b_ref, c_ref, a_ref, init_ref, y_ref, fin_ref, bu_ref, st_ref, *, tc, bsz):
    d = pl.program_id(0)
    c = pl.program_id(2)
    ns = S5_BLOCK_ST

    @pl.when(c == 0)
    def _():
        st_ref[...] = init_ref[0, 0]

    bu_ref[...] = _bdot(u_ref[...].astype(BF16), b_ref[0, 0])
    a_re = jnp.broadcast_to(a_ref[0, 0, 0:1, :], (bsz, ns))
    a_im = jnp.broadcast_to(a_ref[0, 0, 1:2, :], (bsz, ns))

    def body(i, carry):
        s_re, s_im = carry
        t = jnp.where(d == 0, i, tc - 1 - i)
        rows = pl.ds(pl.multiple_of(t * bsz, bsz), bsz)
        n_re = a_re * s_re - a_im * s_im + bu_ref[rows, 0:ns]
        n_im = a_re * s_im + a_im * s_re + bu_ref[rows, ns:2 * ns]
        bu_ref[rows, 0:ns] = n_re
        bu_ref[rows, ns:2 * ns] = n_im
        return n_re, n_im

    s_re, s_im = lax.fori_loop(0, tc, body, (st_ref[:, 0:ns], st_ref[:, ns:2 * ns]))
    st_ref[:, 0:ns] = s_re
    st_ref[:, ns:2 * ns] = s_im
    y_ref[0] = _bdot(bu_ref[...].astype(BF16), c_ref[0, 0])

    @pl.when(c == pl.num_programs(2) - 1)
    def _():
        fin_ref[0, 0] = st_ref[...]


def s5_scan(u_tm, bmat, cmat, abar, init, *, seq, tc=128):
    t_tok, ch = u_tm.shape
    bsz = t_tok // seq
    nj = ch // S5_BLOCK_CH
    tc = min(tc, seq)
    nc = seq // tc
    rows = tc * bsz
    ns2 = 2 * S5_BLOCK_ST
    chunk = lambda d, c: c + d * (nc - 1 - 2 * c)
    par = lambda r, cdim: pl.BlockSpec((1, 1, r, cdim), lambda d, j, c: (d, j, 0, 0))
    return pl.pallas_call(
        functools.partial(_s5_kernel, tc=tc, bsz=bsz),
        grid=(2, nj, nc),
        in_specs=[pl.BlockSpec((rows, S5_BLOCK_CH), lambda d, j, c: (chunk(d, c), j)),
                  par(S5_BLOCK_CH, ns2), par(ns2, S5_BLOCK_CH), par(2, S5_BLOCK_ST), par(bsz, ns2)],
        out_specs=[pl.BlockSpec((1, rows, S5_BLOCK_CH), lambda d, j, c: (d, chunk(d, c), j)),
                   par(bsz, ns2)],
        out_shape=[jax.ShapeDtypeStruct((2, t_tok, ch), F32),
                   jax.ShapeDtypeStruct((2, nj, bsz, ns2), F32)],
        scratch_shapes=[pltpu.VMEM((rows, ns2), F32), pltpu.VMEM((bsz, ns2), F32)],
        compiler_params=_cparams("arbitrary", "arbitrary", "arbitrary"),
        name="s5_scan",
    )(u_tm, bmat, cmat, abar, init)


def _merge_kernel(x_ref, sh_ref, sc_ref, gt_ref, zh_ref, zc_ref, ys_ref, us_ref,
                  wg_ref, why_ref, wcf_ref, ds_ref, wglu_ref, bglu_ref, ws5_ref, wo_ref,
                  lg_ref, lb_ref, o_ref, *, alpha):
    d = x_ref.shape[-1]
    x = x_ref[...]
    h = _ln_mod(x, sh_ref[0], sc_ref[0]).astype(BF16)
    gates = _sigmoid(_bdot(h, wg_ref[...]))
    y_hy = _bdot(zh_ref[...].astype(BF16), why_ref[...])
    y_cf = _bdot(zc_ref[...].astype(BF16), wcf_ref[...])
    ys = ys_ref[0] + ys_ref[1] + ds_ref[...] * us_ref[...]
    g = _gelu_tanh(ys)
    y5 = g * _sigmoid(_bdot(g.astype(BF16), wglu_ref[...]) + bglu_ref[...])
    y_s5 = _bdot(y5.astype(BF16), ws5_ref[...])
    mix = gates[:, 0:d] * y_hy + gates[:, d:2 * d] * y_cf + gates[:, 2 * d:3 * d] * y_s5
    y = _bdot(mix.astype(BF16), wo_ref[...])
    o_ref[...] = _ln(alpha * x + gt_ref[0] * y) * lg_ref[...] + lb_ref[...]


def merge_branches(x2, shift, scale, gate, z_hy, z_cf, ys_tm, us_tm, w_gate, w_hy, w_cf, d_skip,
                   w_glu, b_glu, w_s5, w_o, ln_g, ln_b, *, seq, alpha, tm=256):
    t_tok, d = x2.shape
    bsz = t_tok // seq
    ch = D_BRANCH
    tm = min(tm, seq)
    nt = seq // tm
    mod_spec = pl.BlockSpec((1, 1, d), lambda i: (i // nt, 0, 0))
    tok = lambda n: pl.BlockSpec((tm, n), lambda i: (i, 0))
    ys_v = ys_tm.reshape(2, seq, bsz * ch)
    us_v = us_tm.reshape(seq, bsz * ch)
    return pl.pallas_call(
        functools.partial(_merge_kernel, alpha=alpha),
        grid=(t_tok // tm,),
        in_specs=[tok(d), mod_spec, mod_spec, mod_spec, tok(ch), tok(ch),
                  pl.BlockSpec((2, tm, ch), lambda i: (0, i % nt, i // nt)),
                  pl.BlockSpec((tm, ch), lambda i: (i % nt, i // nt)),
                  _const_spec((d, 3 * d)), _const_spec((ch, d)), _const_spec((ch, d)),
                  _const_spec((1, ch)), _const_spec((ch, ch)), _const_spec((1, ch)),
                  _const_spec((ch, d)), _const_spec((d, d)), _const_spec((1, d)), _const_spec((1, d))],
        out_specs=tok(d),
        out_shape=jax.ShapeDtypeStruct((t_tok, d), F32),
        compiler_params=_cparams("arbitrary"),
        name="merge_branches",
    )(x2, shift, scale, gate, z_hy, z_cf, ys_v, us_v, w_gate, w_hy, w_cf, d_skip, w_glu, b_glu,
      w_s5, w_o, ln_g, ln_b)


def _ffn_kernel(x_ref, sh_ref, sc_ref, gt_ref, w1_ref, w3_ref, w2_ref, lg_ref, lb_ref, o_ref,
                h_ref, acc_ref, *, alpha):
    k = pl.program_id(1)

    @pl.when(k == 0)
    def _():
        h_ref[...] = _ln_mod(x_ref[...], sh_ref[0], sc_ref[0]).astype(BF16)
        acc_ref[...] = jnp.zeros_like(acc_ref)

    h = h_ref[...]
    act = _silu(_bdot(h, w1_ref[...])) * _bdot(h, w3_ref[...])
    acc_ref[...] += _bdot(act.astype(BF16), w2_ref[...])

    @pl.when(k == pl.num_programs(1) - 1)
    def _():
        o_ref[...] = _ln(alpha * x_ref[...] + gt_ref[0] * acc_ref[...]) * lg_ref[...] + lb_ref[...]


def ffn_dense(x2, shift, scale, gate, w1, w3, w2, ln_g, ln_b, *, seq, alpha, tk=256):
    t_tok, d = x2.shape
    dff = w1.shape[1]
    tm = min(seq, 1024)
    nt = seq // tm
    mod_spec = pl.BlockSpec((1, 1, d), lambda i, k: (i // nt, 0, 0))
    vec = pl.BlockSpec((1, d), lambda i, k: (0, 0))
    return pl.pallas_call(
        functools.partial(_ffn_kernel, alpha=alpha),
        grid=(t_tok // tm, dff // tk),
        in_specs=[pl.BlockSpec((tm, d), lambda i, k: (i, 0)), mod_spec, mod_spec, mod_spec,
                  pl.BlockSpec((d, tk), lambda i, k: (0, k)), pl.BlockSpec((d, tk), lambda i, k: (0, k)),
                  pl.BlockSpec((tk, d), lambda i, k: (k, 0)), vec, vec],
        out_specs=pl.BlockSpec((tm, d), lambda i, k: (i, 0)),
        out_shape=jax.ShapeDtypeStruct((t_tok, d), F32),
        scratch_shapes=[pltpu.VMEM((tm, d), BF16), pltpu.VMEM((tm, d), F32)],
        compiler_params=_cparams("arbitrary", "arbitrary"),
        name="ffn_dense",
    )(x2, shift, scale, gate, w1, w3, w2, ln_g, ln_b)


def _router_weights(hf, router):
    logits = jnp.dot(hf, router, precision=HIGHEST, preferred_element_type=F32)
    lane = lax.broadcasted_iota(jnp.int32, logits.shape, 1).astype(F32)
    neg = jnp.float32(-jnp.inf)
    logits = jnp.where(lane < N_EXPERTS, logits, neg)
    m1 = jnp.max(logits, axis=-1, keepdims=True)
    i1 = jnp.min(jnp.where(logits == m1, lane, float(ROUTER_LANES)), axis=-1, keepdims=True)
    rest = jnp.where(lane == i1, neg, logits)
    m2 = jnp.max(rest, axis=-1, keepdims=True)
    i2 = jnp.min(jnp.where(rest == m2, lane, float(ROUTER_LANES)), axis=-1, keepdims=True)
    e2 = jnp.exp(m2 - m1)
    w1 = 1.0 / (1.0 + e2)
    w2 = e2 / (1.0 + e2)
    return jnp.where(lane == i1, w1, 0.0) + jnp.where(lane == i2, w2, 0.0)


def _moe_kernel(x_ref, sh_ref, sc_ref, gt_ref, r_ref, w1_ref, w3_ref, w2_ref, lg_ref, lb_ref, o_ref,
                h_ref, comb_ref, acc_ref, *, alpha):
    e = pl.program_id(1)
    k = pl.program_id(2)

    @pl.when((e == 0) & (k == 0))
    def _():
        hf = _ln_mod(x_ref[...], sh_ref[0], sc_ref[0])
        h_ref[...] = hf.astype(BF16)
        comb_ref[...] = _router_weights(hf, r_ref[...])
        acc_ref[...] = jnp.zeros_like(acc_ref)

    h = h_ref[...]
    act = _silu(_bdot(h, w1_ref[0])) * _bdot(h, w3_ref[0])
    lane = lax.broadcasted_iota(jnp.int32, comb_ref.shape, 1)
    w_e = jnp.sum(jnp.where(lane == e, comb_ref[...], 0.0), axis=-1, keepdims=True)
    acc_ref[...] += w_e * _bdot(act.astype(BF16), w2_ref[0])

    @pl.when((e == pl.num_programs(1) - 1) & (k == pl.num_programs(2) - 1))
    def _():
        o_ref[...] = _ln(alpha * x_ref[...] + gt_ref[0] * acc_ref[...]) * lg_ref[...] + lb_ref[...]


def ffn_moe(x2, shift, scale, gate, router, w1, w3, w2, ln_g, ln_b, *, seq, alpha, tk=512):
    t_tok, d = x2.shape
    n_e, _, dff = w1.shape
    tm = min(seq, 1024)
    nt = seq // tm
    mod_spec = pl.BlockSpec((1, 1, d), lambda i, e, k: (i // nt, 0, 0))
    vec = pl.BlockSpec((1, d), lambda i, e, k: (0, 0))
    return pl.pallas_call(
        functools.partial(_moe_kernel, alpha=alpha),
        grid=(t_tok // tm, n_e, dff // tk),
        in_specs=[pl.BlockSpec((tm, d), lambda i, e, k: (i, 0)), mod_spec, mod_spec, mod_spec,
                  pl.BlockSpec((d, ROUTER_LANES), lambda i, e, k: (0, 0)),
                  pl.BlockSpec((1, d, tk), lambda i, e, k: (e, 0, k)),
                  pl.BlockSpec((1, d, tk), lambda i, e, k: (e, 0, k)),
                  pl.BlockSpec((1, tk, d), lambda i, e, k: (e, k, 0)), vec, vec],
        out_specs=pl.BlockSpec((tm, d), lambda i, e, k: (i, 0)),
        out_shape=jax.ShapeDtypeStruct((t_tok, d), F32),
        scratch_shapes=[pltpu.VMEM((tm, d), BF16), pltpu.VMEM((tm, ROUTER_LANES), F32),
                        pltpu.VMEM((tm, d), F32)],
        compiler_params=_cparams("arbitrary", "arbitrary", "arbitrary"),
        name="ffn_moe",
    )(x2, shift, scale, gate, router, w1, w3, w2, ln_g, ln_b)


def _dft_matrices(seq):
    n = 2 * seq
    f = lax.broadcasted_iota(jnp.int32, (seq, seq), 0)
    t = lax.broadcasted_iota(jnp.int32, (seq, seq), 1)
    ang = ((f * t) % n).astype(F32) * (2.0 * math.pi / n)
    return jnp.cos(ang), jnp.sin(ang)


def _position_features(seq, n_bands, width):
    t = jnp.arange(seq, dtype=F32)[:, None]
    tn = t / max(seq - 1, 1)
    bands = jnp.arange(1, n_bands + 1, dtype=F32)
    ang = t * bands * (2.0 * math.pi / seq)
    z = jnp.concatenate([tn, jnp.cos(ang), jnp.sin(ang)], axis=-1)
    return jnp.pad(z, ((0, 0), (0, width - z.shape[1])))


def _s5_discretize(a_re, a_im, log_dt, b_re, b_im):
    lam_re = jnp.minimum(a_re, S5_MAX_RE)
    lam_im = a_im
    dt = jnp.exp(log_dt)[..., None]
    mag = jnp.exp(lam_re * dt)
    ang = lam_im * dt
    abar_re = mag * jnp.cos(ang)
    abar_im = mag * jnp.sin(ang)
    den = lam_re * lam_re + lam_im * lam_im
    q_re = ((abar_re - 1.0) * lam_re + abar_im * lam_im) / den
    q_im = (abar_im * lam_re - (abar_re - 1.0) * lam_im) / den
    bb_re = q_re[..., None] * b_re - q_im[..., None] * b_im
    bb_im = q_re[..., None] * b_im + q_im[..., None] * b_re
    return abar_re, abar_im, bb_re, bb_im


def _s5_matrices(a_re, a_im, log_dt, b_re, b_im, c_re, c_im):
    abr, abi, bbr, bbi = _s5_discretize(a_re, a_im, log_dt, b_re, b_im)
    g_all, p, k = bbr.shape[1:]
    gb = S5_BLOCK_CH // S5_GROUP
    nj = g_all // gb
    eye = jnp.eye(gb, dtype=F32)

    def in_mat(bb):
        t = jnp.swapaxes(bb, -1, -2).reshape(2, nj, gb, k, p)
        return jnp.einsum("djgkp,gh->djgkhp", t, eye).reshape(2, nj, gb * k, gb * p)

    def out_mat(cc):
        t = jnp.swapaxes(cc, -1, -2).reshape(2, nj, gb, p, k)
        return jnp.einsum("djhpc,hg->djhpgc", t, eye).reshape(2, nj, gb * p, gb * k)

    bmat = jnp.concatenate([in_mat(bbr), in_mat(bbi)], axis=-1).astype(BF16)
    cmat = jnp.concatenate([out_mat(c_re), -out_mat(c_im)], axis=-2).astype(BF16)
    abar = jnp.stack([abr.reshape(2, nj, gb * p), abi.reshape(2, nj, gb * p)], axis=2)
    return bmat, cmat, abar


def kernel(x, c, ctx, c_ctx, w_mod, b_mod, w_in, hy_short_w, hy_short_b, hy_f_w1, hy_f_b1, hy_f_w2, hy_f_b2, hy_f_w3, hy_f_b3, hy_freq, hy_decay, hy_bias, w_hy_out, cf_dw_w, cf_dw_b, cf_ln_g, cf_ln_b, w_cf_out, s5_a_re, s5_a_im, s5_log_dt, s5_b_re, s5_b_im, s5_c_re, s5_c_im, s5_d, s5_w_glu, s5_b_glu, w_s5_out, w_o, ln1_g, ln1_b, ln2_g, ln2_b, ffn_w1, ffn_w3, ffn_w2, moe_router, moe_w1, moe_w3, moe_w2):
    bsz, seq, d = x.shape
    seq_c = ctx.shape[1]
    depth = w_mod.shape[0]
    alpha = (2.0 * depth) ** 0.25
    ch = D_BRANCH
    hy_cols = (HY_ORDER + 1) * ch
    cf_off = hy_cols
    s5_off = cf_off + 2 * ch
    gate_off = s5_off + ch
    n_bands = (hy_f_w1.shape[1] - 1) // 2
    nj = ch // S5_BLOCK_CH

    x2 = x.reshape(bsz * seq, d)
    xc2 = ctx.reshape(bsz * seq_c, d)
    mod_rows = 24
    cc = jnp.zeros((mod_rows, d), F32).at[:bsz].set(c).at[bsz].set(c_ctx)

    dft = {}
    feats = {}
    for s in (seq, seq_c):
        cs, ss = _dft_matrices(s)
        dft[s] = (cs, ss, cs.astype(BF16), ss.astype(BF16))
        feats[s] = _position_features(s, n_bands, 128)
    zero_state = jnp.zeros((2, nj, bsz, 2 * S5_BLOCK_ST), F32)
    row = lambda v: v[None, :]

    for l in range(depth):
        last = l == depth - 1
        mods = mod_vectors(cc, w_mod[l].astype(BF16), row(b_mod[l])).reshape(mod_rows, 6, d)
        lat = [mods[:bsz, i][:, None, :] for i in range(6)]
        cxm = [jnp.broadcast_to(mods[bsz, i][None, None, :], (bsz, 1, d)) for i in range(6)]

        w_in_b = w_in[l].astype(BF16)
        w_hy = w_in_b[:, :cf_off]
        w_cfa = w_in_b[:, cf_off:cf_off + ch]
        w_cfg = w_in_b[:, cf_off + ch:s5_off]
        w_s5 = w_in_b[:, s5_off:gate_off]
        w_gate = w_in_b[:, gate_off:]
        sw = jnp.concatenate([hy_short_w[l][:, i * ch:(i + 1) * ch] for i in range(HY_ORDER + 1)], axis=0)
        sb = hy_short_b[l].reshape(HY_ORDER + 1, ch)
        w1p = jnp.pad(hy_f_w1[l], ((0, 128 - hy_f_w1.shape[1]), (0, 0)))
        bmat, cmat, abar = _s5_matrices(s5_a_re[l], s5_a_im[l], s5_log_dt[l], s5_b_re[l], s5_b_im[l],
                                        s5_c_re[l], s5_c_im[l])
        merge_w = (w_gate, w_hy_out[l].astype(BF16), w_cf_out[l].astype(BF16), row(s5_d[l]),
                   s5_w_glu[l].astype(BF16), row(s5_b_glu[l]), w_s5_out[l].astype(BF16),
                   w_o[l].astype(BF16), row(ln1_g[l]), row(ln1_b[l]))

        def spectra(s):
            filt = hyena_filters(feats[s], w1p, row(hy_f_b1[l]), hy_f_w2[l], row(hy_f_b2[l]), hy_f_w3[l],
                                 row(hy_f_b3[l]), hy_freq[l], row(hy_decay[l]))
            return hyena_spectra(dft[s][0], dft[s][1], filt)

        def token_mixer(xs, s, seg, m, init):
            sh1, sc1, g1 = m[0], m[1], m[2]
            kr, ki, kn = spectra(s)
            p_hy = ln_mod_proj(xs, sh1, sc1, [w_hy], seq=s, tn=512)
            z_hy = hyena_mix(p_hy, sw, sb, dft[s][2], dft[s][3], kr, ki, kn, hy_bias[l], seq=s)
            u_cf = ln_mod_proj(xs, sh1, sc1, [w_cfa, w_cfg], seq=s, tn=512)
            z_cf = conformer_mix(u_cf, cf_dw_w[l], row(cf_dw_b[l]), row(cf_ln_g[l]), row(cf_ln_b[l]), seg=seg)
            u_s5 = ln_mod_proj(xs, sh1, sc1, [w_s5], seq=s, tn=512, time_major=True)
            ys, fin = s5_scan(u_s5, bmat, cmat, abar, init, seq=s)
            x1 = merge_branches(xs, sh1, sc1, g1, z_hy, z_cf, ys, u_s5, *merge_w, seq=s, alpha=alpha)
            return x1, fin

        def channel_mixer(xs, s, m):
            sh2, sc2, g2 = m[3], m[4], m[5]
            i = l // 2
            if l % 2 == 0:
                return ffn_dense(xs, sh2, sc2, g2, ffn_w1[i].astype(BF16), ffn_w3[i].astype(BF16),
                                 ffn_w2[i].astype(BF16), row(ln2_g[l]), row(ln2_b[l]), seq=s, alpha=alpha)
            router = jnp.pad(moe_router[i], ((0, 0), (0, ROUTER_LANES - moe_router.shape[-1])))
            return ffn_moe(xs, sh2, sc2, g2, router, moe_w1[i].astype(BF16), moe_w3[i].astype(BF16),
                           moe_w2[i].astype(BF16), row(ln2_g[l]), row(ln2_b[l]), seq=s, alpha=alpha)

        if last:
            u_s5 = ln_mod_proj(xc2, cxm[0], cxm[1], [w_s5], seq=seq_c, tn=512, time_major=True)
            _, finals = s5_scan(u_s5, bmat, cmat, abar, zero_state, seq=seq_c)
        else:
            xc2, finals = token_mixer(xc2, seq_c, seq_c, cxm, zero_state)
            xc2 = channel_mixer(xc2, seq_c, cxm)

        x2, _ = token_mixer(x2, seq, GRID_W, lat, finals)
        x2 = channel_mixer(x2, seq, lat)
    return x2.reshape(bsz, seq, d)
```

```python
import functools
import math

import jax
import jax.numpy as jnp
from jax import lax
from jax.experimental import pallas as pl
from jax.experimental.pallas import tpu as pltpu

F32 = jnp.float32
BF16 = jnp.bfloat16
HIGHEST = lax.Precision.HIGHEST

V7X_VMEM_LIMIT_BYTES = 56 * 1024 * 1024

LN_EPS = 1e-5
D_BRANCH = 512
HY_ORDER = 2
HY_SHORT = 3
HY_ROW_BLOCK = 512
CF_KERNEL = 31
CF_PAD = 16
GRID_W = 64
S5_GROUP = 16
S5_STATE = 64
S5_MAX_RE = -1e-4
S5_BLOCK_CH = 128
S5_BLOCK_ST = S5_BLOCK_CH // S5_GROUP * S5_STATE
N_EXPERTS = 8
TOP_K = 2
ROUTER_LANES = 128
ROUTE_COLS = 8


def _cparams(*sem):
    return pltpu.CompilerParams(dimension_semantics=sem, vmem_limit_bytes=V7X_VMEM_LIMIT_BYTES)


def _const_spec(shape):
    nd = len(shape)
    return pl.BlockSpec(shape, lambda *_: (0,) * nd, pipeline_mode=pl.Buffered(1))


def _bdot(a, b):
    return jnp.dot(a, b, preferred_element_type=F32)


def _sigmoid(x):
    return 1.0 / (1.0 + jnp.exp(-x))


def _silu(x):
    return x * _sigmoid(x)


def _gelu_tanh(x):
    return 0.5 * x * (1.0 + jnp.tanh(math.sqrt(2.0 / math.pi) * (x + 0.044715 * (x * x * x))))


def _ln(x):
    mu = jnp.mean(x, axis=-1, keepdims=True)
    xc = x - mu
    var = jnp.mean(xc * xc, axis=-1, keepdims=True)
    return xc * lax.rsqrt(var + LN_EPS)


def _ln_mod(x, shift, scale):
    return _ln(x) * (1.0 + scale) + shift


def _mod_kernel(c_ref, w_ref, b_ref, o_ref):
    s = _silu(c_ref[...])
    o_ref[...] = _bdot(s.astype(BF16), w_ref[...]) + b_ref[...]


def mod_vectors(cc, w, b, *, tn=1536):
    r, d = cc.shape
    n = w.shape[1]
    return pl.pallas_call(
        _mod_kernel,
        grid=(n // tn,),
        in_specs=[pl.BlockSpec((r, d), lambda j: (0, 0)),
                  pl.BlockSpec((d, tn), lambda j: (0, j)),
                  pl.BlockSpec((1, tn), lambda j: (0, j))],
        out_specs=pl.BlockSpec((r, tn), lambda j: (0, j)),
        out_shape=jax.ShapeDtypeStruct((r, n), F32),
        compiler_params=_cparams("arbitrary"),
        name="mod_vectors",
    )(cc, w, b)


def _proj_kernel(x_ref, sh_ref, sc_ref, w_ref, o_ref, h_ref):
    @pl.when(pl.program_id(1) == 0)
    def _():
        h_ref[...] = _ln_mod(x_ref[...], sh_ref[0], sc_ref[0]).astype(BF16)

    o_ref[...] = _bdot(h_ref[...], w_ref[...])


def ln_mod_proj(x2, shift, scale, w, *, seq, tn=512):
    t_tok, d = x2.shape
    n = w.shape[1]
    tm = min(seq, 1024)
    nt = seq // tm
    mod_spec = pl.BlockSpec((1, 1, d), lambda i, j: (i // nt, 0, 0))
    return pl.pallas_call(
        _proj_kernel,
        grid=(t_tok // tm, n // tn),
        in_specs=[pl.BlockSpec((tm, d), lambda i, j: (i, 0)), mod_spec, mod_spec,
                  pl.BlockSpec((d, tn), lambda i, j: (0, j))],
        out_specs=pl.BlockSpec((tm, tn), lambda i, j: (i, j)),
        out_shape=jax.ShapeDtypeStruct((t_tok, n), F32),
        scratch_shapes=[pltpu.VMEM((tm, d), BF16)],
        compiler_params=_cparams("arbitrary", "arbitrary"),
        name="ln_mod_proj",
    )(x2, shift, scale, w)


def _filter_kernel(z_ref, w1_ref, b1_ref, w2_ref, b2_ref, w3_ref, b3_ref, fr_ref, dec_ref, o_ref):
    z = z_ref[...]
    h = jnp.sin(fr_ref[0:1, :] * (jnp.dot(z, w1_ref[...], precision=HIGHEST, preferred_element_type=F32)
                                  + b1_ref[...]))
    h = jnp.sin(fr_ref[1:2, :] * (jnp.dot(h, w2_ref[...], precision=HIGHEST, preferred_element_type=F32)
                                  + b2_ref[...]))
    h = jnp.dot(h, w3_ref[...], precision=HIGHEST, preferred_element_type=F32) + b3_ref[...]
    tn = z[:, 0:1]
    o_ref[...] = h * jnp.exp(-tn * jnp.abs(dec_ref[...]))


def hyena_filters(z, w1, b1, w2, b2, w3, b3, freq, decay, *, tn=512):
    seq, zf = z.shape
    hid = w2.shape[0]
    n = w3.shape[1]
    full = lambda shape: pl.BlockSpec(shape, lambda j: (0, 0))
    return pl.pallas_call(
        _filter_kernel,
        grid=(n // tn,),
        in_specs=[full((seq, zf)), full((zf, hid)), full((1, hid)), full((hid, hid)), full((1, hid)),
                  pl.BlockSpec((hid, tn), lambda j: (0, j)), pl.BlockSpec((1, tn), lambda j: (0, j)),
                  full((2, hid)), pl.BlockSpec((1, tn), lambda j: (0, j))],
        out_specs=pl.BlockSpec((seq, tn), lambda j: (0, j)),
        out_shape=jax.ShapeDtypeStruct((seq, n), F32),
        compiler_params=_cparams("arbitrary"),
        name="hyena_filters",
    )(z, w1, b1, w2, b2, w3, b3, freq, decay)


def _spectrum_kernel(cs_ref, ss_ref, filt_ref, kr_ref, ki_ref, kn_ref, *, tf, n_fft):
    ch = kr_ref.shape[-1]
    seq = filt_ref.shape[0]
    hf = filt_ref[:, 0:ch]
    tau = lax.broadcasted_iota(jnp.int32, (seq, 1), 0)
    hb = jnp.where(tau == 0, 0.0, filt_ref[:, ch:2 * ch])
    s_cos = hf + hb
    s_sin = hb - hf
    f = pl.program_id(1) * tf + lax.broadcasted_iota(jnp.int32, (tf, 1), 0)
    wgt = jnp.where(f == 0, 1.0, 2.0) / n_fft
    kr_ref[0] = wgt * jnp.dot(cs_ref[...], s_cos, precision=HIGHEST, preferred_element_type=F32)
    ki_ref[0] = wgt * jnp.dot(ss_ref[...], s_sin, precision=HIGHEST, preferred_element_type=F32)
    sign = (1 - 2 * (tau & 1)).astype(F32)
    kn_ref[0] = jnp.sum(sign * s_cos, axis=0, keepdims=True) / n_fft


def hyena_spectra(cs, ss, filt, *, tf=256):
    seq = cs.shape[0]
    ch = D_BRANCH
    tf = min(tf, seq)
    return pl.pallas_call(
        functools.partial(_spectrum_kernel, tf=tf, n_fft=2 * seq),
        grid=(HY_ORDER, seq // tf),
        in_specs=[pl.BlockSpec((tf, seq), lambda o, f: (f, 0)),
                  pl.BlockSpec((tf, seq), lambda o, f: (f, 0)),
                  pl.BlockSpec((seq, 2 * ch), lambda o, f: (0, o))],
        out_specs=[pl.BlockSpec((1, tf, ch), lambda o, f: (o, f, 0)),
                   pl.BlockSpec((1, tf, ch), lambda o, f: (o, f, 0)),
                   pl.BlockSpec((1, 1, ch), lambda o, f: (o, 0, 0))],
        out_shape=[jax.ShapeDtypeStruct((HY_ORDER, seq, ch), F32),
                   jax.ShapeDtypeStruct((HY_ORDER, seq, ch), F32),
                   jax.ShapeDtypeStruct((HY_ORDER, 1, ch), F32)],
        compiler_params=_cparams("arbitrary", "arbitrary"),
        name="hyena_spectra",
    )(cs, ss, filt)


def _hyena_kernel(x1_ref, x2_ref, v_ref, sw_ref, sb_ref, cs_ref, ss_ref, kr_ref, ki_ref, kn_ref,
                  bias_ref, z_ref, zb_ref, zr_ref, zi_ref, xs_ref):
    seq = v_ref.shape[0]
    row = lax.broadcasted_iota(jnp.int32, (seq, 1), 0)
    sign = (1 - 2 * (row & 1)).astype(F32)

    def short_conv(p_ref, part):
        p = p_ref[...]
        prev = jnp.where(row == 0, 0.0, pltpu.roll(p, 1, 0))
        nxt = jnp.where(row == seq - 1, 0.0, pltpu.roll(p, seq - 1, 0))
        w = sw_ref[part * HY_SHORT:(part + 1) * HY_SHORT, :]
        return w[0:1] * prev + w[1:2] * p + w[2:3] * nxt + sb_ref[part:part + 1, :]

    rb = min(seq, HY_ROW_BLOCK)
    blocks = [slice(r * rb, (r + 1) * rb) for r in range(seq // rb)]
    z_ref[...] = short_conv(v_ref, 2)
    for o, x_ref in enumerate((x1_ref, x2_ref)):
        z = z_ref[...]
        zb_ref[...] = z.astype(BF16)
        nyq = jnp.sum(sign * z, axis=0, keepdims=True) * kn_ref[o]
        xs_ref[...] = short_conv(x_ref, o)
        for rows in blocks:
            a = _bdot(cs_ref[rows, :], zb_ref[...])
            b = _bdot(ss_ref[rows, :], zb_ref[...])
            kr = kr_ref[o, rows, :]
            ki = ki_ref[o, rows, :]
            zr_ref[rows, :] = (a * kr + b * ki).astype(BF16)
            zi_ref[rows, :] = (a * ki - b * kr).astype(BF16)
        for rows in blocks:
            y = _bdot(cs_ref[rows, :], zr_ref[...]) - _bdot(ss_ref[rows, :], zi_ref[...])
            y = y + sign[rows] * nyq + bias_ref[o:o + 1, :] * z_ref[rows, :]
            z_ref[rows, :] = xs_ref[rows, :] * y


def hyena_mix(p_hy, sw, sb, cs, ss, kr, ki, kn, bias, *, seq, cb=256):
    t_tok = p_hy.shape[0]
    ch = D_BRANCH
    bsz = t_tok // seq
    ncb = ch // cb
    col = lambda part: pl.BlockSpec((seq, cb), lambda j, b: (b, part * ncb + j))
    per_c = lambda rows: pl.BlockSpec((rows, cb), lambda j, b: (0, j))
    spec3 = lambda rows: pl.BlockSpec((HY_ORDER, rows, cb), lambda j, b: (0, 0, j),
                                      pipeline_mode=pl.Buffered(1))
    return pl.pallas_call(
        _hyena_kernel,
        grid=(ncb, bsz),
        in_specs=[col(0), col(1), col(2), per_c(3 * HY_SHORT), per_c(3),
                  _const_spec((seq, seq)), _const_spec((seq, seq)),
                  spec3(seq), spec3(seq), spec3(1), per_c(HY_ORDER)],
        out_specs=pl.BlockSpec((seq, cb), lambda j, b: (b, j)),
        out_shape=jax.ShapeDtypeStruct((t_tok, ch), F32),
        scratch_shapes=[pltpu.VMEM((seq, cb), BF16), pltpu.VMEM((seq, cb), BF16),
                        pltpu.VMEM((seq, cb), BF16), pltpu.VMEM((seq, cb), F32)],
        compiler_params=_cparams("arbitrary", "arbitrary"),
        name="hyena_mix",
    )(p_hy, p_hy, p_hy, sw, sb, cs, ss, kr, ki, kn, bias)


def _conformer_kernel(a_ref, gl_ref, w_ref, b_ref, g_ref, be_ref, o_ref, pad_ref, *, seg, nseg):
    ch = a_ref.shape[-1]
    zeros = jnp.zeros((nseg, CF_PAD, ch), F32)
    pad_ref[:, 0:CF_PAD, :] = zeros
    pad_ref[:, CF_PAD + seg:, :] = zeros
    u = a_ref[...] * _sigmoid(gl_ref[...])
    pad_ref[:, CF_PAD:CF_PAD + seg, :] = u.reshape(nseg, seg, ch)
    half = (CF_KERNEL - 1) // 2
    acc = jnp.zeros((nseg, seg, ch), F32)
    for k in range(CF_KERNEL):
        start = CF_PAD - half + k
        acc = acc + w_ref[k:k + 1, :] * pad_ref[:, start:start + seg, :]
    y = _ln(acc + b_ref[...]) * g_ref[...] + be_ref[...]
    o_ref[...] = _silu(y).reshape(nseg * seg, ch)


def conformer_mix(p, dw_w, dw_b, ln_g, ln_b, *, seg, col_block):
    t_tok = p.shape[0]
    ch = D_BRANCH
    tm = max(seg, 512)
    nseg = tm // seg
    vec = lambda rows: pl.BlockSpec((rows, ch), lambda i: (0, 0))
    return pl.pallas_call(
        functools.partial(_conformer_kernel, seg=seg, nseg=nseg),
        grid=(t_tok // tm,),
        in_specs=[pl.BlockSpec((tm, ch), lambda i: (i, col_block)),
                  pl.BlockSpec((tm, ch), lambda i: (i, col_block + 1)),
                  vec(CF_KERNEL), vec(1), vec(1), vec(1)],
        out_specs=pl.BlockSpec((tm, ch), lambda i: (i, 0)),
        out_shape=jax.ShapeDtypeStruct((t_tok, ch), F32),
        scratch_shapes=[pltpu.VMEM((nseg, seg + 2 * CF_PAD, ch), F32)],
        compiler_params=_cparams("arbitrary"),
        name="conformer_mix",
    )(p, p, dw_w, dw_b, ln_g, ln_b)


def _s5_kernel(u_ref, b_ref, c_ref, a_ref, init_ref, y_ref, fin_ref, u1_ref, utm_ref, bu_ref, st_ref,
               *, tc, bsz):
    d = pl.program_id(0)
    c = pl.program_id(2)
    ns = S5_BLOCK_ST

    @pl.when(c == 0)
    def _():
        st_ref[...] = init_ref[0, 0]

    u1_ref[...] = u_ref[...].reshape(bsz * tc, S5_BLOCK_CH)

    def to_time_major(t, carry):
        utm_ref[pl.ds(pl.multiple_of(t * bsz, bsz), bsz), :] = u1_ref[pl.ds(t, bsz, stride=tc), :]
        return carry

    lax.fori_loop(0, tc, to_time_major, 0, unroll=8)
    bu_ref[...] = _bdot(utm_ref[...].astype(BF16), b_ref[0, 0])
    a_re = jnp.broadcast_to(a_ref[0, 0, 0:1, :], (bsz, ns))
    a_im = jnp.broadcast_to(a_ref[0, 0, 1:2, :], (bsz, ns))

    def body(i, carry):
        s_re, s_im = carry
        t = jnp.where(d == 0, i, tc - 1 - i)
        rows = pl.ds(pl.multiple_of(t * bsz, bsz), bsz)
        n_re = a_re * s_re - a_im * s_im + bu_ref[rows, 0:ns]
        n_im = a_re * s_im + a_im * s_re + bu_ref[rows, ns:2 * ns]
        bu_ref[rows, 0:ns] = n_re
        bu_ref[rows, ns:2 * ns] = n_im
        return n_re, n_im

    s_re, s_im = lax.fori_loop(0, tc, body, (st_ref[:, 0:ns], st_ref[:, ns:2 * ns]))
    st_ref[:, 0:ns] = s_re
    st_ref[:, ns:2 * ns] = s_im
    utm_ref[...] = _bdot(bu_ref[...].astype(BF16), c_ref[0, 0])
    for b in range(bsz):
        y_ref[0, b] = utm_ref[pl.ds(b, tc, stride=bsz), :]

    @pl.when(c == pl.num_programs(2) - 1)
    def _():
        fin_ref[0, 0] = st_ref[...]


def s5_scan(p, bmat, cmat, abar, init, *, seq, col_off, tc=128):
    t_tok, n_cols = p.shape
    ch = D_BRANCH
    bsz = t_tok // seq
    u3 = p.reshape(bsz, seq, n_cols)
    j0 = col_off // S5_BLOCK_CH
    nj = ch // S5_BLOCK_CH
    tc = min(tc, seq)
    nc = seq // tc
    rows = tc * bsz
    ns2 = 2 * S5_BLOCK_ST
    chunk = lambda d, c: c + d * (nc - 1 - 2 * c)
    par = lambda r, cdim: pl.BlockSpec((1, 1, r, cdim), lambda d, j, c: (d, j, 0, 0))
    y, fin = pl.pallas_call(
        functools.partial(_s5_kernel, tc=tc, bsz=bsz),
        grid=(2, nj, nc),
        in_specs=[pl.BlockSpec((bsz, tc, S5_BLOCK_CH), lambda d, j, c: (0, chunk(d, c), j0 + j)),
                  par(S5_BLOCK_CH, ns2), par(ns2, S5_BLOCK_CH), par(2, S5_BLOCK_ST), par(bsz, ns2)],
        out_specs=[pl.BlockSpec((1, bsz, tc, S5_BLOCK_CH), lambda d, j, c: (d, 0, chunk(d, c), j)),
                   par(bsz, ns2)],
        out_shape=[jax.ShapeDtypeStruct((2, bsz, seq, ch), F32),
                   jax.ShapeDtypeStruct((2, nj, bsz, ns2), F32)],
        scratch_shapes=[pltpu.VMEM((rows, S5_BLOCK_CH), F32), pltpu.VMEM((rows, S5_BLOCK_CH), F32),
                        pltpu.VMEM((rows, ns2), F32), pltpu.VMEM((bsz, ns2), F32)],
        compiler_params=_cparams("arbitrary", "arbitrary", "arbitrary"),
        name="s5_scan",
    )(u3, bmat, cmat, abar, init)
    return y.reshape(2, t_tok, ch), fin


def _merge_kernel(x_ref, sh_ref, sc_ref, gt_ref, zh_ref, zc_ref, ys_ref, us_ref,
                  wg_ref, why_ref, wcf_ref, ds_ref, wglu_ref, bglu_ref, ws5_ref, wo_ref,
                  lg_ref, lb_ref, o_ref, *, alpha):
    d = x_ref.shape[-1]
    x = x_ref[...]
    h = _ln_mod(x, sh_ref[0], sc_ref[0]).astype(BF16)
    gates = _sigmoid(_bdot(h, wg_ref[...]))
    y_hy = _bdot(zh_ref[...].astype(BF16), why_ref[...])
    y_cf = _bdot(zc_ref[...].astype(BF16), wcf_ref[...])
    ys = ys_ref[0] + ys_ref[1] + ds_ref[...] * us_ref[...]
    g = _gelu_tanh(ys)
    y5 = g * _sigmoid(_bdot(g.astype(BF16), wglu_ref[...]) + bglu_ref[...])
    y_s5 = _bdot(y5.astype(BF16), ws5_ref[...])
    mix = gates[:, 0:d] * y_hy + gates[:, d:2 * d] * y_cf + gates[:, 2 * d:3 * d] * y_s5
    y = _bdot(mix.astype(BF16), wo_ref[...])
    o_ref[...] = _ln(alpha * x + gt_ref[0] * y) * lg_ref[...] + lb_ref[...]


def merge_branches(x2, shift, scale, gate, z_hy, z_cf, ys, p, w_gate, w_hy, w_cf, d_skip,
                   w_glu, b_glu, w_s5, w_o, ln_g, ln_b, *, seq, alpha, s5_col_block, tm=256):
    t_tok, d = x2.shape
    ch = D_BRANCH
    tm = min(tm, seq)
    nt = seq // tm
    mod_spec = pl.BlockSpec((1, 1, d), lambda i: (i // nt, 0, 0))
    tok = lambda n: pl.BlockSpec((tm, n), lambda i: (i, 0))
    return pl.pallas_call(
        functools.partial(_merge_kernel, alpha=alpha),
        grid=(t_tok // tm,),
        in_specs=[tok(d), mod_spec, mod_spec, mod_spec, tok(ch), tok(ch),
                  pl.BlockSpec((2, tm, ch), lambda i: (0, i, 0)),
                  pl.BlockSpec((tm, ch), lambda i: (i, s5_col_block)),
                  _const_spec((d, 3 * d)), _const_spec((ch, d)), _const_spec((ch, d)),
                  _const_spec((1, ch)), _const_spec((ch, ch)), _const_spec((1, ch)),
                  _const_spec((ch, d)), _const_spec((d, d)), _const_spec((1, d)), _const_spec((1, d))],
        out_specs=tok(d),
        out_shape=jax.ShapeDtypeStruct((t_tok, d), F32),
        compiler_params=_cparams("arbitrary"),
        name="merge_branches",
    )(x2, shift, scale, gate, z_hy, z_cf, ys, p, w_gate, w_hy, w_cf, d_skip, w_glu, b_glu,
      w_s5, w_o, ln_g, ln_b)


def _ffn_kernel(x_ref, sh_ref, sc_ref, gt_ref, w1_ref, w3_ref, w2_ref, lg_ref, lb_ref, o_ref,
                h_ref, acc_ref, *, alpha):
    k = pl.program_id(1)

    @pl.when(k == 0)
    def _():
        h_ref[...] = _ln_mod(x_ref[...], sh_ref[0], sc_ref[0]).astype(BF16)
        acc_ref[...] = jnp.zeros_like(acc_ref)

    h = h_ref[...]
    act = _silu(_bdot(h, w1_ref[...])) * _bdot(h, w3_ref[...])
    acc_ref[...] += _bdot(act.astype(BF16), w2_ref[...])

    @pl.when(k == pl.num_programs(1) - 1)
    def _():
        o_ref[...] = _ln(alpha * x_ref[...] + gt_ref[0] * acc_ref[...]) * lg_ref[...] + lb_ref[...]


def ffn_dense(x2, shift, scale, gate, w1, w3, w2, ln_g, ln_b, *, seq, alpha, tk=256):
    t_tok, d = x2.shape
    dff = w1.shape[1]
    tm = min(seq, 1024)
    nt = seq // tm
    mod_spec = pl.BlockSpec((1, 1, d), lambda i, k: (i // nt, 0, 0))
    vec = pl.BlockSpec((1, d), lambda i, k: (0, 0))
    return pl.pallas_call(
        functools.partial(_ffn_kernel, alpha=alpha),
        grid=(t_tok // tm, dff // tk),
        in_specs=[pl.BlockSpec((tm, d), lambda i, k: (i, 0)), mod_spec, mod_spec, mod_spec,
                  pl.BlockSpec((d, tk), lambda i, k: (0, k)), pl.BlockSpec((d, tk), lambda i, k: (0, k)),
                  pl.BlockSpec((tk, d), lambda i, k: (k, 0)), vec, vec],
        out_specs=pl.BlockSpec((tm, d), lambda i, k: (i, 0)),
        out_shape=jax.ShapeDtypeStruct((t_tok, d), F32),
        scratch_shapes=[pltpu.VMEM((tm, d), BF16), pltpu.VMEM((tm, d), F32)],
        compiler_params=_cparams("arbitrary", "arbitrary"),
        name="ffn_dense",
    )(x2, shift, scale, gate, w1, w3, w2, ln_g, ln_b)


def _router_top2(hf, router):
    logits = jnp.dot(hf, router, precision=HIGHEST, preferred_element_type=F32)
    lane = lax.broadcasted_iota(jnp.int32, logits.shape, 1).astype(F32)
    neg = jnp.float32(-jnp.inf)
    logits = jnp.where(lane < N_EXPERTS, logits, neg)
    m1 = jnp.max(logits, axis=-1, keepdims=True)
    i1 = jnp.min(jnp.where(logits == m1, lane, float(ROUTER_LANES)), axis=-1, keepdims=True)
    rest = jnp.where(lane == i1, neg, logits)
    m2 = jnp.max(rest, axis=-1, keepdims=True)
    i2 = jnp.min(jnp.where(rest == m2, lane, float(ROUTER_LANES)), axis=-1, keepdims=True)
    e2 = jnp.exp(m2 - m1)
    return i1, i2, 1.0 / (1.0 + e2), e2 / (1.0 + e2)


def _route_kernel(x_ref, sh_ref, sc_ref, r_ref, o_ref):
    hf = _ln_mod(x_ref[...], sh_ref[0], sc_ref[0])
    i1, i2, w1, w2 = _router_top2(hf, r_ref[...])
    lane = lax.broadcasted_iota(jnp.int32, o_ref.shape, 1)
    o_ref[...] = jnp.where(lane == 0, i1, jnp.where(lane == 1, i2, jnp.where(lane == 2, w1,
                           jnp.where(lane == 3, w2, 0.0))))


def moe_route(x2, shift, scale, router, *, seq):
    t_tok, d = x2.shape
    tm = min(seq, 1024)
    nt = seq // tm
    mod_spec = pl.BlockSpec((1, 1, d), lambda i: (i // nt, 0, 0))
    return pl.pallas_call(
        _route_kernel,
        grid=(t_tok // tm,),
        in_specs=[pl.BlockSpec((tm, d), lambda i: (i, 0)), mod_spec, mod_spec,
                  pl.BlockSpec((d, ROUTER_LANES), lambda i: (0, 0))],
        out_specs=pl.BlockSpec((tm, ROUTE_COLS), lambda i: (i, 0)),
        out_shape=jax.ShapeDtypeStruct((t_tok, ROUTE_COLS), F32),
        compiler_params=_cparams("arbitrary"),
        name="moe_route",
    )(x2, shift, scale, router)


def _dispatch_kernel(pos_ref, x_ref, sh_ref, sc_ref, xs_in_ref, xs_ref, hbuf, sem, *, tm):
    del xs_in_ref
    i = pl.program_id(0)
    n = pl.num_programs(0)
    slot = i % 2

    def row_copy(s, r, p):
        return pltpu.make_async_copy(hbuf.at[s, pl.ds(r, 1)], xs_ref.at[pl.ds(p, 1)], sem.at[s])

    def wait_slot(s):
        for _ in range(TOP_K):
            pltpu.make_async_copy(hbuf.at[s], xs_ref.at[pl.ds(0, tm)], sem.at[s]).wait()

    @pl.when(i >= 2)
    def _():
        wait_slot(slot)

    hbuf[slot] = _ln_mod(x_ref[...], sh_ref[0], sc_ref[0])

    def body(r, carry):
        base = (i * tm + r) * TOP_K
        for k in range(TOP_K):
            row_copy(slot, r, pos_ref[base + k]).start()
        return carry

    lax.fori_loop(0, tm, body, 0, unroll=8)

    @pl.when(i == n - 1)
    def _():
        wait_slot(slot)

    @pl.when((i == n - 1) & (i >= 1))
    def _():
        wait_slot(1 - slot)


def moe_dispatch(pos, x2, shift, scale, n_rows, *, seq, tm=512):
    t_tok, d = x2.shape
    tm = min(tm, seq)
    nt = seq // tm
    mod_spec = pl.BlockSpec((1, 1, d), lambda i, p: (i // nt, 0, 0))
    return pl.pallas_call(
        functools.partial(_dispatch_kernel, tm=tm),
        grid_spec=pltpu.PrefetchScalarGridSpec(
            num_scalar_prefetch=1,
            grid=(t_tok // tm,),
            in_specs=[pl.BlockSpec((tm, d), lambda i, p: (i, 0)), mod_spec, mod_spec,
                      pl.BlockSpec(memory_space=pl.ANY)],
            out_specs=pl.BlockSpec(memory_space=pl.ANY),
            scratch_shapes=[pltpu.VMEM((2, tm, d), F32), pltpu.SemaphoreType.DMA((2,))]),
        out_shape=jax.ShapeDtypeStruct((n_rows, d), F32),
        input_output_aliases={4: 0},
        compiler_params=_cparams("arbitrary"),
        name="moe_dispatch",
    )(pos, x2, shift, scale, jnp.zeros((n_rows, d), F32))


def _expert_kernel(te_ref, na_ref, xs_ref, w1_ref, w3_ref, w2_ref, o_ref, xb_ref):
    t = pl.program_id(0)
    k = pl.program_id(1)
    active = t < na_ref[0]

    @pl.when(active)
    def _():
        @pl.when(k == 0)
        def _():
            xb_ref[...] = xs_ref[...].astype(BF16)

        xb = xb_ref[...]
        act = _silu(_bdot(xb, w1_ref[0])) * _bdot(xb, w3_ref[0])
        y = _bdot(act.astype(BF16), w2_ref[0])

        @pl.when(k == 0)
        def _():
            o_ref[...] = y

        @pl.when(k > 0)
        def _():
            o_ref[...] += y

    @pl.when(jnp.logical_not(active) & (k == 0))
    def _():
        o_ref[...] = jnp.zeros_like(o_ref)


def moe_experts(tile_expert, n_active, xs, w1, w3, w2, *, tg, tk=512):
    n_rows, d = xs.shape
    dff = w1.shape[-1]
    nk = dff // tk
    kk = lambda t, k, na: jnp.where(t < na[0], k, nk - 1)
    return pl.pallas_call(
        _expert_kernel,
        grid_spec=pltpu.PrefetchScalarGridSpec(
            num_scalar_prefetch=2,
            grid=(n_rows // tg, nk),
            in_specs=[pl.BlockSpec((tg, d), lambda t, k, te, na: (t, 0)),
                      pl.BlockSpec((1, d, tk), lambda t, k, te, na: (te[t], 0, kk(t, k, na))),
                      pl.BlockSpec((1, d, tk), lambda t, k, te, na: (te[t], 0, kk(t, k, na))),
                      pl.BlockSpec((1, tk, d), lambda t, k, te, na: (te[t], kk(t, k, na), 0))],
            out_specs=pl.BlockSpec((tg, d), lambda t, k, te, na: (t, 0)),
            scratch_shapes=[pltpu.VMEM((tg, d), BF16)]),
        out_shape=jax.ShapeDtypeStruct((n_rows, d), F32),
        compiler_params=_cparams("arbitrary", "arbitrary"),
        name="moe_experts",
    )(tile_expert, n_active, xs, w1, w3, w2)


def _combine_kernel(pos_ref, x_ref, gt_ref, rt_ref, lg_ref, lb_ref, ys_ref, o_ref, ybuf, sem, *, tm, alpha):
    i = pl.program_id(0)
    n = pl.num_programs(0)
    slot = i % 2

    def issue(step, s):
        def body(r, carry):
            base = (step * tm + r) * TOP_K
            for k in range(TOP_K):
                pltpu.make_async_copy(ys_ref.at[pl.ds(pos_ref[base + k], 1)], ybuf.at[s, k, pl.ds(r, 1)],
                                      sem.at[s]).start()
            return carry
        lax.fori_loop(0, tm, body, 0, unroll=8)

    @pl.when(i == 0)
    def _():
        issue(0, 0)

    @pl.when(i + 1 < n)
    def _():
        issue(i + 1, 1 - slot)

    for k in range(TOP_K):
        pltpu.make_async_copy(ys_ref.at[pl.ds(0, tm)], ybuf.at[slot, k], sem.at[slot]).wait()
    rt = rt_ref[...]
    y = rt[:, 2:3] * ybuf[slot, 0] + rt[:, 3:4] * ybuf[slot, 1]
    o_ref[...] = _ln(alpha * x_ref[...] + gt_ref[0] * y) * lg_ref[...] + lb_ref[...]


def moe_combine(pos, x2, gate, route, ln_g, ln_b, ys, *, seq, alpha, tm=512):
    t_tok, d = x2.shape
    tm = min(tm, seq)
    nt = seq // tm
    vec = pl.BlockSpec((1, d), lambda i, p: (0, 0))
    return pl.pallas_call(
        functools.partial(_combine_kernel, tm=tm, alpha=alpha),
        grid_spec=pltpu.PrefetchScalarGridSpec(
            num_scalar_prefetch=1,
            grid=(t_tok // tm,),
            in_specs=[pl.BlockSpec((tm, d), lambda i, p: (i, 0)),
                      pl.BlockSpec((1, 1, d), lambda i, p: (i // nt, 0, 0)),
                      pl.BlockSpec((tm, ROUTE_COLS), lambda i, p: (i, 0)), vec, vec,
                      pl.BlockSpec(memory_space=pl.ANY)],
            out_specs=pl.BlockSpec((tm, d), lambda i, p: (i, 0)),
            scratch_shapes=[pltpu.VMEM((2, TOP_K, tm, d), F32), pltpu.SemaphoreType.DMA((2,))]),
        out_shape=jax.ShapeDtypeStruct((t_tok, d), F32),
        compiler_params=_cparams("arbitrary"),
        name="moe_combine",
    )(pos, x2, gate, route, ln_g, ln_b, ys)


def _dispatch_plan(route, *, tg):
    ids = route[:, 0:TOP_K].astype(jnp.int32).reshape(-1)
    n_assign = ids.shape[0]
    onehot = (ids[:, None] == jnp.arange(N_EXPERTS, dtype=jnp.int32)[None, :]).astype(jnp.int32)
    csum = jnp.cumsum(onehot, axis=0)
    counts = csum[-1]
    padded = (counts + tg - 1) // tg * tg
    ends = jnp.cumsum(padded)
    starts = ends - padded
    pos = jnp.sum(onehot * (starts[None, :] + csum - 1), axis=1).astype(jnp.int32)
    n_tiles = n_assign // tg + N_EXPERTS
    tile_start = jnp.arange(n_tiles, dtype=jnp.int32) * tg
    tile_expert = jnp.minimum(jnp.sum(tile_start[:, None] >= ends[None, :], axis=1), N_EXPERTS - 1)
    n_active = (ends[-1] // tg).reshape(1)
    return pos, tile_expert.astype(jnp.int32), n_active.astype(jnp.int32), n_tiles * tg


def ffn_moe(x2, shift, scale, gate, router, w1, w3, w2, ln_g, ln_b, *, seq, alpha, tg=1024):
    route = moe_route(x2, shift, scale, router, seq=seq)
    pos, tile_expert, n_active, n_rows = _dispatch_plan(route, tg=tg)
    xs = moe_dispatch(pos, x2, shift, scale, n_rows, seq=seq)
    ys = moe_experts(tile_expert, n_active, xs, w1, w3, w2, tg=tg)
    return moe_combine(pos, x2, gate, route, ln_g, ln_b, ys, seq=seq, alpha=alpha)


def _dft_matrices(seq):
    n = 2 * seq
    f = lax.broadcasted_iota(jnp.int32, (seq, seq), 0)
    t = lax.broadcasted_iota(jnp.int32, (seq, seq), 1)
    ang = ((f * t) % n).astype(F32) * (2.0 * math.pi / n)
    return jnp.cos(ang), jnp.sin(ang)


def _position_features(seq, n_bands, width):
    t = jnp.arange(seq, dtype=F32)[:, None]
    tn = t / max(seq - 1, 1)
    bands = jnp.arange(1, n_bands + 1, dtype=F32)
    ang = t * bands * (2.0 * math.pi / seq)
    z = jnp.concatenate([tn, jnp.cos(ang), jnp.sin(ang)], axis=-1)
    return jnp.pad(z, ((0, 0), (0, width - z.shape[1])))


def _s5_discretize(a_re, a_im, log_dt, b_re, b_im):
    lam_re = jnp.minimum(a_re, S5_MAX_RE)
    lam_im = a_im
    dt = jnp.exp(log_dt)[..., None]
    mag = jnp.exp(lam_re * dt)
    ang = lam_im * dt
    abar_re = mag * jnp.cos(ang)
    abar_im = mag * jnp.sin(ang)
    den = lam_re * lam_re + lam_im * lam_im
    q_re = ((abar_re - 1.0) * lam_re + abar_im * lam_im) / den
    q_im = (abar_im * lam_re - (abar_re - 1.0) * lam_im) / den
    bb_re = q_re[..., None] * b_re - q_im[..., None] * b_im
    bb_im = q_re[..., None] * b_im + q_im[..., None] * b_re
    return abar_re, abar_im, bb_re, bb_im


def _s5_matrices(a_re, a_im, log_dt, b_re, b_im, c_re, c_im):
    abr, abi, bbr, bbi = _s5_discretize(a_re, a_im, log_dt, b_re, b_im)
    g_all, p, k = bbr.shape[1:]
    gb = S5_BLOCK_CH // S5_GROUP
    nj = g_all // gb
    eye = jnp.eye(gb, dtype=F32)

    def in_mat(bb):
        t = jnp.swapaxes(bb, -1, -2).reshape(2, nj, gb, k, p)
        return jnp.einsum("djgkp,gh->djgkhp", t, eye).reshape(2, nj, gb * k, gb * p)

    def out_mat(cc):
        t = jnp.swapaxes(cc, -1, -2).reshape(2, nj, gb, p, k)
        return jnp.einsum("djhpc,hg->djhpgc", t, eye).reshape(2, nj, gb * p, gb * k)

    bmat = jnp.concatenate([in_mat(bbr), in_mat(bbi)], axis=-1).astype(BF16)
    cmat = jnp.concatenate([out_mat(c_re), -out_mat(c_im)], axis=-2).astype(BF16)
    abar = jnp.stack([abr.reshape(2, nj, gb * p), abi.reshape(2, nj, gb * p)], axis=2)
    return bmat, cmat, abar


def kernel(x, c, ctx, c_ctx, w_mod, b_mod, w_in, hy_short_w, hy_short_b, hy_f_w1, hy_f_b1, hy_f_w2, hy_f_b2, hy_f_w3, hy_f_b3, hy_freq, hy_decay, hy_bias, w_hy_out, cf_dw_w, cf_dw_b, cf_ln_g, cf_ln_b, w_cf_out, s5_a_re, s5_a_im, s5_log_dt, s5_b_re, s5_b_im, s5_c_re, s5_c_im, s5_d, s5_w_glu, s5_b_glu, w_s5_out, w_o, ln1_g, ln1_b, ln2_g, ln2_b, ffn_w1, ffn_w3, ffn_w2, moe_router, moe_w1, moe_w3, moe_w2):
    bsz, seq, d = x.shape
    seq_c = ctx.shape[1]
    depth = w_mod.shape[0]
    alpha = (2.0 * depth) ** 0.25
    ch = D_BRANCH
    hy_cols = (HY_ORDER + 1) * ch
    cf_off = hy_cols
    s5_off = cf_off + 2 * ch
    gate_off = s5_off + ch
    n_bands = (hy_f_w1.shape[1] - 1) // 2
    nj = ch // S5_BLOCK_CH

    x2 = x.reshape(bsz * seq, d)
    xc2 = ctx.reshape(bsz * seq_c, d)
    mod_rows = 24
    cc = jnp.zeros((mod_rows, d), F32).at[:bsz].set(c).at[bsz].set(c_ctx)

    dft = {}
    feats = {}
    for s in (seq, seq_c):
        cs, ss = _dft_matrices(s)
        dft[s] = (cs, ss, cs.astype(BF16), ss.astype(BF16))
        feats[s] = _position_features(s, n_bands, 128)
    zero_state = jnp.zeros((2, nj, bsz, 2 * S5_BLOCK_ST), F32)
    row = lambda v: v[None, :]

    for l in range(depth):
        last = l == depth - 1
        mods = mod_vectors(cc, w_mod[l].astype(BF16), row(b_mod[l])).reshape(mod_rows, 6, d)
        lat = [mods[:bsz, i][:, None, :] for i in range(6)]
        cxm = [jnp.broadcast_to(mods[bsz, i][None, None, :], (bsz, 1, d)) for i in range(6)]

        w_in_b = w_in[l].astype(BF16)
        w_mix = w_in_b[:, :gate_off]
        w_s5 = w_in_b[:, s5_off:gate_off]
        w_gate = w_in_b[:, gate_off:]
        sw = jnp.concatenate([hy_short_w[l][:, i * ch:(i + 1) * ch] for i in range(HY_ORDER + 1)], axis=0)
        sb = hy_short_b[l].reshape(HY_ORDER + 1, ch)
        w1p = jnp.pad(hy_f_w1[l], ((0, 128 - hy_f_w1.shape[1]), (0, 0)))
        bmat, cmat, abar = _s5_matrices(s5_a_re[l], s5_a_im[l], s5_log_dt[l], s5_b_re[l], s5_b_im[l],
                                        s5_c_re[l], s5_c_im[l])
        merge_w = (w_gate, w_hy_out[l].astype(BF16), w_cf_out[l].astype(BF16), row(s5_d[l]),
                   s5_w_glu[l].astype(BF16), row(s5_b_glu[l]), w_s5_out[l].astype(BF16),
                   w_o[l].astype(BF16), row(ln1_g[l]), row(ln1_b[l]))

        def spectra(s):
            filt = hyena_filters(feats[s], w1p, row(hy_f_b1[l]), hy_f_w2[l], row(hy_f_b2[l]), hy_f_w3[l],
                                 row(hy_f_b3[l]), hy_freq[l], row(hy_decay[l]))
            return hyena_spectra(dft[s][0], dft[s][1], filt)

        def token_mixer(xs, s, seg, m, init):
            sh1, sc1, g1 = m[0], m[1], m[2]
            kr, ki, kn = spectra(s)
            p = ln_mod_proj(xs, sh1, sc1, w_mix, seq=s)
            z_hy = hyena_mix(p, sw, sb, dft[s][2], dft[s][3], kr, ki, kn, hy_bias[l], seq=s)
            z_cf = conformer_mix(p, cf_dw_w[l], row(cf_dw_b[l]), row(cf_ln_g[l]), row(cf_ln_b[l]), seg=seg,
                                 col_block=cf_off // ch)
            ys, fin = s5_scan(p, bmat, cmat, abar, init, seq=s, col_off=s5_off)
            x1 = merge_branches(xs, sh1, sc1, g1, z_hy, z_cf, ys, p, *merge_w, seq=s, alpha=alpha,
                                s5_col_block=s5_off // ch)
            return x1, fin

        def channel_mixer(xs, s, m):
            sh2, sc2, g2 = m[3], m[4], m[5]
            i = l // 2
            if l % 2 == 0:
                return ffn_dense(xs, sh2, sc2, g2, ffn_w1[i].astype(BF16), ffn_w3[i].astype(BF16),
                                 ffn_w2[i].astype(BF16), row(ln2_g[l]), row(ln2_b[l]), seq=s, alpha=alpha)
            router = jnp.pad(moe_router[i], ((0, 0), (0, ROUTER_LANES - moe_router.shape[-1])))
            return ffn_moe(xs, sh2, sc2, g2, router, moe_w1[i].astype(BF16), moe_w3[i].astype(BF16),
                           moe_w2[i].astype(BF16), row(ln2_g[l]), row(ln2_b[l]), seq=s, alpha=alpha)

        if last:
            u_s5 = ln_mod_proj(xc2, cxm[0], cxm[1], w_s5, seq=seq_c)
            _, finals = s5_scan(u_s5, bmat, cmat, abar, zero_state, seq=seq_c, col_off=0)
        else:
            xc2, finals = token_mixer(xc2, seq_c, seq_c, cxm, zero_state)
            xc2 = channel_mixer(xc2, seq_c, cxm)

        x2, _ = token_mixer(x2, seq, GRID_W, lat, finals)
        x2 = channel_mixer(x2, seq, lat)
    return x2.reshape(bsz, seq, d)
```

```python
import functools
import math

import jax
import jax.numpy as jnp
from jax import lax
from jax.experimental import pallas as pl
from jax.experimental.pallas import tpu as pltpu

F32 = jnp.float32
BF16 = jnp.bfloat16
HIGHEST = lax.Precision.HIGHEST

V7X_VMEM_LIMIT_BYTES = 56 * 1024 * 1024

LN_EPS = 1e-5
D_BRANCH = 512
HY_ORDER = 2
HY_SHORT = 3
HY_ROW_BLOCK = 512
CF_KERNEL = 31
SUBLANES = 8
LANES = 128
CF_PAD = 16
CF_SHIFT_SPAN = 24
GRID_W = 64
S5_GROUP = 16
S5_STATE = 64
S5_MAX_RE = -1e-4
S5_TIME_BLOCK = 16
S5_BLOCK_CH = 128
S5_BLOCK_ST = S5_BLOCK_CH // S5_GROUP * S5_STATE
N_EXPERTS = 8
TOP_K = 2
ROUTER_LANES = 128
ROUTE_COLS = 8


def _cparams(*sem):
    return pltpu.CompilerParams(dimension_semantics=sem, vmem_limit_bytes=V7X_VMEM_LIMIT_BYTES)


def _const_spec(shape):
    nd = len(shape)
    return pl.BlockSpec(shape, lambda *_: (0,) * nd, pipeline_mode=pl.Buffered(1))


def _bdot(a, b):
    return jnp.dot(a, b, preferred_element_type=F32)


def _sigmoid(x):
    return 1.0 / (1.0 + jnp.exp(-x))


def _silu(x):
    return x * _sigmoid(x)


def _gelu_tanh(x):
    return 0.5 * x * (1.0 + jnp.tanh(math.sqrt(2.0 / math.pi) * (x + 0.044715 * (x * x * x))))


def _ln(x):
    mu = jnp.mean(x, axis=-1, keepdims=True)
    xc = x - mu
    var = jnp.mean(xc * xc, axis=-1, keepdims=True)
    return xc * lax.rsqrt(var + LN_EPS)


def _ln_mod(x, shift, scale):
    return _ln(x) * (1.0 + scale) + shift


def _mod_kernel(c_ref, w_ref, b_ref, o_ref):
    s = _silu(c_ref[...])
    o_ref[...] = _bdot(s.astype(BF16), w_ref[...]) + b_ref[...]


def mod_vectors(cc, w, b, *, tn=1536):
    r, d = cc.shape
    n = w.shape[1]
    return pl.pallas_call(
        _mod_kernel,
        grid=(n // tn,),
        in_specs=[pl.BlockSpec((r, d), lambda j: (0, 0)),
                  pl.BlockSpec((d, tn), lambda j: (0, j)),
                  pl.BlockSpec((1, tn), lambda j: (0, j))],
        out_specs=pl.BlockSpec((r, tn), lambda j: (0, j)),
        out_shape=jax.ShapeDtypeStruct((r, n), F32),
        compiler_params=_cparams("arbitrary"),
        name="mod_vectors",
    )(cc, w, b)


def _proj_kernel(x_ref, sh_ref, sc_ref, w_ref, o_ref, *, tn):
    h = _ln_mod(x_ref[...], sh_ref[0], sc_ref[0]).astype(BF16)
    for j in range(o_ref.shape[1] // tn):
        cols = slice(j * tn, (j + 1) * tn)
        o_ref[:, cols] = _bdot(h, w_ref[:, cols])


def ln_mod_proj(x2, shift, scale, w, *, seq, tn=512):
    t_tok, d = x2.shape
    n = w.shape[1]
    tm = min(seq, 1024)
    nt = seq // tm
    mod_spec = pl.BlockSpec((1, 1, d), lambda i: (i // nt, 0, 0))
    return pl.pallas_call(
        functools.partial(_proj_kernel, tn=tn),
        grid=(t_tok // tm,),
        in_specs=[pl.BlockSpec((tm, d), lambda i: (i, 0)), mod_spec, mod_spec, _const_spec((d, n))],
        out_specs=pl.BlockSpec((tm, n), lambda i: (i, 0)),
        out_shape=jax.ShapeDtypeStruct((t_tok, n), F32),
        compiler_params=_cparams("arbitrary"),
        name="ln_mod_proj",
    )(x2, shift, scale, w)


def _filter_kernel(z_ref, w1_ref, b1_ref, w2_ref, b2_ref, w3_ref, b3_ref, fr_ref, dec_ref, o_ref):
    z = z_ref[...]
    h = jnp.sin(fr_ref[0:1, :] * (jnp.dot(z, w1_ref[...], precision=HIGHEST, preferred_element_type=F32)
                                  + b1_ref[...]))
    h = jnp.sin(fr_ref[1:2, :] * (jnp.dot(h, w2_ref[...], precision=HIGHEST, preferred_element_type=F32)
                                  + b2_ref[...]))
    h = jnp.dot(h, w3_ref[...], precision=HIGHEST, preferred_element_type=F32) + b3_ref[...]
    tn = z[:, 0:1]
    o_ref[...] = h * jnp.exp(-tn * jnp.abs(dec_ref[...]))


def hyena_filters(z, w1, b1, w2, b2, w3, b3, freq, decay, *, tn=512):
    seq, zf = z.shape
    hid = w2.shape[0]
    n = w3.shape[1]
    full = lambda shape: pl.BlockSpec(shape, lambda j: (0, 0))
    return pl.pallas_call(
        _filter_kernel,
        grid=(n // tn,),
        in_specs=[full((seq, zf)), full((zf, hid)), full((1, hid)), full((hid, hid)), full((1, hid)),
                  pl.BlockSpec((hid, tn), lambda j: (0, j)), pl.BlockSpec((1, tn), lambda j: (0, j)),
                  full((2, hid)), pl.BlockSpec((1, tn), lambda j: (0, j))],
        out_specs=pl.BlockSpec((seq, tn), lambda j: (0, j)),
        out_shape=jax.ShapeDtypeStruct((seq, n), F32),
        compiler_params=_cparams("arbitrary"),
        name="hyena_filters",
    )(z, w1, b1, w2, b2, w3, b3, freq, decay)


def _spectrum_kernel(cs_ref, ss_ref, filt_ref, kr_ref, ki_ref, kn_ref, *, tf, n_fft):
    ch = kr_ref.shape[-1]
    seq = filt_ref.shape[0]
    hf = filt_ref[:, 0:ch]
    tau = lax.broadcasted_iota(jnp.int32, (seq, 1), 0)
    hb = jnp.where(tau == 0, 0.0, filt_ref[:, ch:2 * ch])
    s_cos = hf + hb
    s_sin = hb - hf
    f = pl.program_id(1) * tf + lax.broadcasted_iota(jnp.int32, (tf, 1), 0)
    wgt = jnp.where(f == 0, 1.0, 2.0) / n_fft
    kr_ref[0] = wgt * jnp.dot(cs_ref[...], s_cos, precision=HIGHEST, preferred_element_type=F32)
    ki_ref[0] = wgt * jnp.dot(ss_ref[...], s_sin, precision=HIGHEST, preferred_element_type=F32)
    sign = (1 - 2 * (tau & 1)).astype(F32)
    kn_ref[0] = jnp.sum(sign * s_cos, axis=0, keepdims=True) / n_fft


def hyena_spectra(cs, ss, filt, *, tf=256):
    seq = cs.shape[0]
    ch = D_BRANCH
    tf = min(tf, seq)
    return pl.pallas_call(
        functools.partial(_spectrum_kernel, tf=tf, n_fft=2 * seq),
        grid=(HY_ORDER, seq // tf),
        in_specs=[pl.BlockSpec((tf, seq), lambda o, f: (f, 0)),
                  pl.BlockSpec((tf, seq), lambda o, f: (f, 0)),
                  pl.BlockSpec((seq, 2 * ch), lambda o, f: (0, o))],
        out_specs=[pl.BlockSpec((1, tf, ch), lambda o, f: (o, f, 0)),
                   pl.BlockSpec((1, tf, ch), lambda o, f: (o, f, 0)),
                   pl.BlockSpec((1, 1, ch), lambda o, f: (o, 0, 0))],
        out_shape=[jax.ShapeDtypeStruct((HY_ORDER, seq, ch), F32),
                   jax.ShapeDtypeStruct((HY_ORDER, seq, ch), F32),
                   jax.ShapeDtypeStruct((HY_ORDER, 1, ch), F32)],
        compiler_params=_cparams("arbitrary", "arbitrary"),
        name="hyena_spectra",
    )(cs, ss, filt)


def _hyena_kernel(*refs, nh):
    parts = [refs[p * nh:(p + 1) * nh] for p in range(HY_ORDER + 1)]
    (sw_ref, sb_ref, cs_ref, ss_ref, tw_ref, k1r_ref, k1i_ref, k2r_ref, k2i_ref, km_ref, bias_ref,
     o_ref, ze_ref, zo_ref, xe_ref, xo_ref, zeb_ref, zob_ref, g_ref, il_ref) = refs[(HY_ORDER + 1) * nh:]
    m = o_ref.shape[0] // 2
    row = lax.broadcasted_iota(jnp.int32, (m, 1), 0)
    sign = (1 - 2 * (row & 1)).astype(F32)

    def short_conv(part):
        pe = jnp.concatenate([r[pl.ds(0, m, stride=2), :] for r in parts[part]], axis=1)
        po = jnp.concatenate([r[pl.ds(1, m, stride=2), :] for r in parts[part]], axis=1)
        po_prev = jnp.where(row == 0, 0.0, pltpu.roll(po, 1, 0))
        pe_next = jnp.where(row == m - 1, 0.0, pltpu.roll(pe, m - 1, 0))
        w = sw_ref[part * HY_SHORT:(part + 1) * HY_SHORT, :]
        b = sb_ref[part:part + 1, :]
        return (w[0:1] * po_prev + w[1:2] * pe + w[2:3] * po + b,
                w[0:1] * pe + w[1:2] * po + w[2:3] * pe_next + b)

    rb = min(m, HY_ROW_BLOCK)
    blocks = [slice(r * rb, (r + 1) * rb) for r in range(m // rb)]
    ze_ref[...], zo_ref[...] = short_conv(HY_ORDER)
    for o in range(HY_ORDER):
        ze = ze_ref[...]
        zo = zo_ref[...]
        zeb_ref[...] = ze.astype(BF16)
        zob_ref[...] = zo.astype(BF16)
        e_m = jnp.sum(sign * ze, axis=0, keepdims=True)
        o_m = jnp.sum(sign * zo, axis=0, keepdims=True)
        kmr = km_ref[o, 0:1, :]
        kmi = km_ref[o, 1:2, :]
        nyq_e = e_m * kmr + o_m * kmi
        nyq_o = o_m * kmr - e_m * kmi
        bias = bias_ref[o:o + 1, :]
        xe_ref[...], xo_ref[...] = short_conv(o)
        for rows in blocks:
            cs = cs_ref[rows, :]
            ss = ss_ref[rows, :]
            ae = _bdot(cs, zeb_ref[...])
            be = _bdot(ss, zeb_ref[...])
            ao = _bdot(cs, zob_ref[...])
            bo = _bdot(ss, zob_ref[...])
            c = tw_ref[0, rows, :]
            s = tw_ref[1, rows, :]
            tr = c * ao - s * bo
            ti = -(c * bo + s * ao)
            pr, pi = ae + tr, ti - be
            qr, qi = ae - tr, -be - ti
            k1r, k1i = k1r_ref[o, rows, :], k1i_ref[o, rows, :]
            k2r, k2i = k2r_ref[o, rows, :], k2i_ref[o, rows, :]
            z1r = pr * k1r - pi * k1i
            z1i = pr * k1i + pi * k1r
            z2r = qr * k2r + qi * k2i
            z2i = qr * k2i - qi * k2r
            dr, di = z1r - z2r, z1i + z2i
            g_ref[0, rows, :] = (z1r + z2r).astype(BF16)
            g_ref[1, rows, :] = (z1i - z2i).astype(BF16)
            g_ref[2, rows, :] = (c * dr - s * di).astype(BF16)
            g_ref[3, rows, :] = (c * di + s * dr).astype(BF16)
        for rows in blocks:
            cs = cs_ref[rows, :]
            ss = ss_ref[rows, :]
            ye = _bdot(cs, g_ref[0]) - _bdot(ss, g_ref[1]) + sign[rows] * nyq_e + bias * ze_ref[rows, :]
            yo = _bdot(cs, g_ref[2]) - _bdot(ss, g_ref[3]) + sign[rows] * nyq_o + bias * zo_ref[rows, :]
            ze_ref[rows, :] = xe_ref[rows, :] * ye
            zo_ref[rows, :] = xo_ref[rows, :] * yo
    for h in range(nh):
        lanes = slice(h * LANES, (h + 1) * LANES)
        il_ref[pl.ds(0, m, stride=2), :] = ze_ref[:, lanes]
        il_ref[pl.ds(1, m, stride=2), :] = zo_ref[:, lanes]
        o_ref[:, lanes] = il_ref[...]


def hyena_mix(p, sw, sb, cs, ss, tw, k1r, k1i, k2r, k2i, km, bias, *, seq, cb=256):
    t_tok = p.shape[0]
    ch = D_BRANCH
    bsz = t_tok // seq
    m = seq // 2
    nh = cb // LANES
    col_blocks = ch // LANES
    cols = [pl.BlockSpec((seq, LANES), functools.partial(lambda j, b, c0: (b, c0 + j * nh), c0=part * col_blocks + h))
            for part in range(HY_ORDER + 1) for h in range(nh)]
    per_c = lambda rows: pl.BlockSpec((rows, cb), lambda j, b: (0, j))
    spec3 = lambda lead, rows: pl.BlockSpec((lead, rows, cb), lambda j, b: (0, 0, j),
                                            pipeline_mode=pl.Buffered(1))
    return pl.pallas_call(
        functools.partial(_hyena_kernel, nh=nh),
        grid=(ch // cb, bsz),
        in_specs=cols + [per_c(3 * HY_SHORT), per_c(3), _const_spec((m, m)), _const_spec((m, m)),
                         spec3(2, m)] + [spec3(HY_ORDER, m)] * 4 + [spec3(HY_ORDER, 2), per_c(HY_ORDER)],
        out_specs=pl.BlockSpec((seq, cb), lambda j, b: (b, j)),
        out_shape=jax.ShapeDtypeStruct((t_tok, ch), F32),
        scratch_shapes=[pltpu.VMEM((m, cb), F32)] * 4 + [pltpu.VMEM((m, cb), BF16)] * 2
                       + [pltpu.VMEM((4, m, cb), BF16), pltpu.VMEM((seq, LANES), F32)],
        compiler_params=_cparams("arbitrary", "arbitrary"),
        name="hyena_mix",
    )(*([p] * ((HY_ORDER + 1) * nh)), sw, sb, cs, ss, tw, k1r, k1i, k2r, k2i, km, bias)


def _conformer_kernel(a_ref, gl_ref, w_ref, b_ref, g_ref, be_ref, o_ref, pad_ref, shift_ref, *, seg, nseg):
    ch = a_ref.shape[-1]
    zeros = jnp.zeros((nseg, CF_PAD, ch), F32)
    pad_ref[:, 0:CF_PAD, :] = zeros
    pad_ref[:, CF_PAD + seg:, :] = zeros
    u = a_ref[...] * _sigmoid(gl_ref[...])
    pad_ref[:, CF_PAD:CF_PAD + seg, :] = u.reshape(nseg, seg, ch)
    span = seg + CF_SHIFT_SPAN
    for r in range(SUBLANES):
        shift_ref[r] = pad_ref[:, r:r + span, :]
    half = (CF_KERNEL - 1) // 2
    for c in range(ch // LANES):
        lanes = slice(c * LANES, (c + 1) * LANES)
        for s in range(nseg):
            acc = jnp.zeros((seg, LANES), F32)
            for k in range(CF_KERNEL):
                q, r = divmod(CF_PAD - half + k, SUBLANES)
                acc = acc + w_ref[k:k + 1, lanes] * shift_ref[r, s, q * SUBLANES:q * SUBLANES + seg, lanes]
            o_ref[s * seg:(s + 1) * seg, lanes] = acc
    y = _ln(o_ref[...] + b_ref[...]) * g_ref[...] + be_ref[...]
    o_ref[...] = _silu(y)


def conformer_mix(p, dw_w, dw_b, ln_g, ln_b, *, seg, col_block):
    t_tok = p.shape[0]
    ch = D_BRANCH
    tm = max(seg, 512)
    nseg = tm // seg
    vec = lambda rows: pl.BlockSpec((rows, ch), lambda i: (0, 0))
    return pl.pallas_call(
        functools.partial(_conformer_kernel, seg=seg, nseg=nseg),
        grid=(t_tok // tm,),
        in_specs=[pl.BlockSpec((tm, ch), lambda i: (i, col_block)),
                  pl.BlockSpec((tm, ch), lambda i: (i, col_block + 1)),
                  vec(CF_KERNEL), vec(1), vec(1), vec(1)],
        out_specs=pl.BlockSpec((tm, ch), lambda i: (i, 0)),
        out_shape=jax.ShapeDtypeStruct((t_tok, ch), F32),
        scratch_shapes=[pltpu.VMEM((nseg, seg + 2 * CF_PAD, ch), F32),
                        pltpu.VMEM((SUBLANES, nseg, seg + CF_SHIFT_SPAN, ch), F32)],
        compiler_params=_cparams("arbitrary"),
        name="conformer_mix",
    )(p, p, dw_w, dw_b, ln_g, ln_b)


def _s5_kernel(u_ref, b_ref, c_ref, a_ref, init_ref, y_ref, fin_ref, u1_ref, utm_ref, bu_ref, st_ref,
               *, tc, bsz):
    d = pl.program_id(0)
    c = pl.program_id(2)
    ns = S5_BLOCK_ST

    @pl.when(c == 0)
    def _():
        st_ref[...] = init_ref[0, 0]

    tb = S5_TIME_BLOCK
    for th in range(tc // tb):
        for b in range(bsz):
            u1_ref[pl.ds((th * bsz + b) * tb, tb), :] = u_ref[b, th * tb:(th + 1) * tb, :]
    for th in range(tc // tb):
        for tl in range(tb):
            utm_ref[pl.ds((th * tb + tl) * bsz, bsz), :] = u1_ref[pl.ds(th * bsz * tb + tl, bsz, stride=tb), :]
    bu_ref[...] = _bdot(utm_ref[...].astype(BF16), b_ref[0, 0])
    a_re = jnp.broadcast_to(a_ref[0, 0, 0:1, :], (bsz, ns))
    a_im = jnp.broadcast_to(a_ref[0, 0, 1:2, :], (bsz, ns))

    def body(i, carry):
        s_re, s_im = carry
        t = jnp.where(d == 0, i, tc - 1 - i)
        rows = pl.ds(pl.multiple_of(t * bsz, bsz), bsz)
        n_re = a_re * s_re - a_im * s_im + bu_ref[rows, 0:ns]
        n_im = a_re * s_im + a_im * s_re + bu_ref[rows, ns:2 * ns]
        bu_ref[rows, 0:ns] = n_re
        bu_ref[rows, ns:2 * ns] = n_im
        return n_re, n_im

    s_re, s_im = lax.fori_loop(0, tc, body, (st_ref[:, 0:ns], st_ref[:, ns:2 * ns]))
    st_ref[:, 0:ns] = s_re
    st_ref[:, ns:2 * ns] = s_im
    utm_ref[...] = _bdot(bu_ref[...].astype(BF16), c_ref[0, 0])
    for b in range(bsz):
        y_ref[0, b] = utm_ref[pl.ds(b, tc, stride=bsz), :]

    @pl.when(c == pl.num_programs(2) - 1)
    def _():
        fin_ref[0, 0] = st_ref[...]


def s5_scan(p, bmat, cmat, abar, init, *, seq, col_off, tc=128):
    t_tok, n_cols = p.shape
    ch = D_BRANCH
    bsz = t_tok // seq
    u3 = p.reshape(bsz, seq, n_cols)
    j0 = col_off // S5_BLOCK_CH
    nj = ch // S5_BLOCK_CH
    tc = min(tc, seq)
    nc = seq // tc
    rows = tc * bsz
    ns2 = 2 * S5_BLOCK_ST
    chunk = lambda d, c: c + d * (nc - 1 - 2 * c)
    par = lambda r, cdim: pl.BlockSpec((1, 1, r, cdim), lambda d, j, c: (d, j, 0, 0))
    y, fin = pl.pallas_call(
        functools.partial(_s5_kernel, tc=tc, bsz=bsz),
        grid=(2, nj, nc),
        in_specs=[pl.BlockSpec((bsz, tc, S5_BLOCK_CH), lambda d, j, c: (0, chunk(d, c), j0 + j)),
                  par(S5_BLOCK_CH, ns2), par(ns2, S5_BLOCK_CH), par(2, S5_BLOCK_ST), par(bsz, ns2)],
        out_specs=[pl.BlockSpec((1, bsz, tc, S5_BLOCK_CH), lambda d, j, c: (d, 0, chunk(d, c), j)),
                   par(bsz, ns2)],
        out_shape=[jax.ShapeDtypeStruct((2, bsz, seq, ch), F32),
                   jax.ShapeDtypeStruct((2, nj, bsz, ns2), F32)],
        scratch_shapes=[pltpu.VMEM((rows, S5_BLOCK_CH), F32), pltpu.VMEM((rows, S5_BLOCK_CH), F32),
                        pltpu.VMEM((rows, ns2), F32), pltpu.VMEM((bsz, ns2), F32)],
        compiler_params=_cparams("arbitrary", "arbitrary", "arbitrary"),
        name="s5_scan",
    )(u3, bmat, cmat, abar, init)
    return y.reshape(2, t_tok, ch), fin


def _merge_kernel(x_ref, sh_ref, sc_ref, gt_ref, zh_ref, zc_ref, ys_ref, us_ref,
                  wg_ref, why_ref, wcf_ref, ds_ref, wglu_ref, bglu_ref, ws5_ref, wo_ref,
                  lg_ref, lb_ref, o_ref, *, alpha):
    d = x_ref.shape[-1]
    x = x_ref[...]
    h = _ln_mod(x, sh_ref[0], sc_ref[0]).astype(BF16)
    gate = lambda i: _sigmoid(_bdot(h, wg_ref[:, i * d:(i + 1) * d]))
    mix = gate(0) * _bdot(zh_ref[...].astype(BF16), why_ref[...])
    mix = mix + gate(1) * _bdot(zc_ref[...].astype(BF16), wcf_ref[...])
    ys = ys_ref[0] + ys_ref[1] + ds_ref[...] * us_ref[...]
    g = _gelu_tanh(ys)
    y5 = g * _sigmoid(_bdot(g.astype(BF16), wglu_ref[...]) + bglu_ref[...])
    mix = mix + gate(2) * _bdot(y5.astype(BF16), ws5_ref[...])
    y = _bdot(mix.astype(BF16), wo_ref[...])
    o_ref[...] = _ln(alpha * x + gt_ref[0] * y) * lg_ref[...] + lb_ref[...]


def merge_branches(x2, shift, scale, gate, z_hy, z_cf, ys, p, w_gate, w_hy, w_cf, d_skip,
                   w_glu, b_glu, w_s5, w_o, ln_g, ln_b, *, seq, alpha, s5_col_block, tm=512):
    t_tok, d = x2.shape
    ch = D_BRANCH
    tm = min(tm, seq)
    nt = seq // tm
    mod_spec = pl.BlockSpec((1, 1, d), lambda i: (i // nt, 0, 0))
    tok = lambda n: pl.BlockSpec((tm, n), lambda i: (i, 0))
    return pl.pallas_call(
        functools.partial(_merge_kernel, alpha=alpha),
        grid=(t_tok // tm,),
        in_specs=[tok(d), mod_spec, mod_spec, mod_spec, tok(ch), tok(ch),
                  pl.BlockSpec((2, tm, ch), lambda i: (0, i, 0)),
                  pl.BlockSpec((tm, ch), lambda i: (i, s5_col_block)),
                  _const_spec((d, 3 * d)), _const_spec((ch, d)), _const_spec((ch, d)),
                  _const_spec((1, ch)), _const_spec((ch, ch)), _const_spec((1, ch)),
                  _const_spec((ch, d)), _const_spec((d, d)), _const_spec((1, d)), _const_spec((1, d))],
        out_specs=tok(d),
        out_shape=jax.ShapeDtypeStruct((t_tok, d), F32),
        compiler_params=_cparams("arbitrary"),
        name="merge_branches",
    )(x2, shift, scale, gate, z_hy, z_cf, ys, p, w_gate, w_hy, w_cf, d_skip, w_glu, b_glu,
      w_s5, w_o, ln_g, ln_b)


def _ffn_kernel(x_ref, sh_ref, sc_ref, gt_ref, w1_ref, w3_ref, w2_ref, lg_ref, lb_ref, o_ref,
                h_ref, acc_ref, *, alpha):
    k = pl.program_id(1)

    @pl.when(k == 0)
    def _():
        h_ref[...] = _ln_mod(x_ref[...], sh_ref[0], sc_ref[0]).astype(BF16)
        acc_ref[...] = jnp.zeros_like(acc_ref)

    h = h_ref[...]
    act = _silu(_bdot(h, w1_ref[...])) * _bdot(h, w3_ref[...])
    acc_ref[...] += _bdot(act.astype(BF16), w2_ref[...])

    @pl.when(k == pl.num_programs(1) - 1)
    def _():
        o_ref[...] = _ln(alpha * x_ref[...] + gt_ref[0] * acc_ref[...]) * lg_ref[...] + lb_ref[...]


def ffn_dense(x2, shift, scale, gate, w1, w3, w2, ln_g, ln_b, *, seq, alpha, tk=256):
    t_tok, d = x2.shape
    dff = w1.shape[1]
    tm = min(seq, 1024)
    nt = seq // tm
    mod_spec = pl.BlockSpec((1, 1, d), lambda i, k: (i // nt, 0, 0))
    vec = pl.BlockSpec((1, d), lambda i, k: (0, 0))
    return pl.pallas_call(
        functools.partial(_ffn_kernel, alpha=alpha),
        grid=(t_tok // tm, dff // tk),
        in_specs=[pl.BlockSpec((tm, d), lambda i, k: (i, 0)), mod_spec, mod_spec, mod_spec,
                  pl.BlockSpec((d, tk), lambda i, k: (0, k)), pl.BlockSpec((d, tk), lambda i, k: (0, k)),
                  pl.BlockSpec((tk, d), lambda i, k: (k, 0)), vec, vec],
        out_specs=pl.BlockSpec((tm, d), lambda i, k: (i, 0)),
        out_shape=jax.ShapeDtypeStruct((t_tok, d), F32),
        scratch_shapes=[pltpu.VMEM((tm, d), BF16), pltpu.VMEM((tm, d), F32)],
        compiler_params=_cparams("arbitrary", "arbitrary"),
        name="ffn_dense",
    )(x2, shift, scale, gate, w1, w3, w2, ln_g, ln_b)


def _router_top2(hf, router):
    logits = jnp.dot(hf, router, precision=HIGHEST, preferred_element_type=F32)
    lane = lax.broadcasted_iota(jnp.int32, logits.shape, 1).astype(F32)
    neg = jnp.float32(-jnp.inf)
    logits = jnp.where(lane < N_EXPERTS, logits, neg)
    m1 = jnp.max(logits, axis=-1, keepdims=True)
    i1 = jnp.min(jnp.where(logits == m1, lane, float(ROUTER_LANES)), axis=-1, keepdims=True)
    rest = jnp.where(lane == i1, neg, logits)
    m2 = jnp.max(rest, axis=-1, keepdims=True)
    i2 = jnp.min(jnp.where(rest == m2, lane, float(ROUTER_LANES)), axis=-1, keepdims=True)
    e2 = jnp.exp(m2 - m1)
    return i1, i2, 1.0 / (1.0 + e2), e2 / (1.0 + e2)


def _route_kernel(x_ref, sh_ref, sc_ref, r_ref, o_ref):
    hf = _ln_mod(x_ref[...], sh_ref[0], sc_ref[0])
    i1, i2, w1, w2 = _router_top2(hf, r_ref[...])
    lane = lax.broadcasted_iota(jnp.int32, o_ref.shape, 1)
    o_ref[...] = jnp.where(lane == 0, i1, jnp.where(lane == 1, i2, jnp.where(lane == 2, w1,
                           jnp.where(lane == 3, w2, 0.0))))


def moe_route(x2, shift, scale, router, *, seq):
    t_tok, d = x2.shape
    tm = min(seq, 1024)
    nt = seq // tm
    mod_spec = pl.BlockSpec((1, 1, d), lambda i: (i // nt, 0, 0))
    return pl.pallas_call(
        _route_kernel,
        grid=(t_tok // tm,),
        in_specs=[pl.BlockSpec((tm, d), lambda i: (i, 0)), mod_spec, mod_spec,
                  pl.BlockSpec((d, ROUTER_LANES), lambda i: (0, 0))],
        out_specs=pl.BlockSpec((tm, ROUTE_COLS), lambda i: (i, 0)),
        out_shape=jax.ShapeDtypeStruct((t_tok, ROUTE_COLS), F32),
        compiler_params=_cparams("arbitrary"),
        name="moe_route",
    )(x2, shift, scale, router)


def _dispatch_kernel(pos_ref, x_ref, sh_ref, sc_ref, xs_in_ref, xs_ref, hbuf, sem, *, tm):
    del xs_in_ref
    i = pl.program_id(0)
    n = pl.num_programs(0)
    slot = i % 2

    def row_copy(s, r, p):
        return pltpu.make_async_copy(hbuf.at[s, pl.ds(r, 1)], xs_ref.at[pl.ds(p, 1)], sem.at[s])

    def wait_slot(s):
        for _ in range(TOP_K):
            pltpu.make_async_copy(hbuf.at[s], xs_ref.at[pl.ds(0, tm)], sem.at[s]).wait()

    @pl.when(i >= 2)
    def _():
        wait_slot(slot)

    hbuf[slot] = _ln_mod(x_ref[...], sh_ref[0], sc_ref[0])

    def body(r, carry):
        base = (i * tm + r) * TOP_K
        for k in range(TOP_K):
            row_copy(slot, r, pos_ref[base + k]).start()
        return carry

    lax.fori_loop(0, tm, body, 0, unroll=8)

    @pl.when(i == n - 1)
    def _():
        wait_slot(slot)

    @pl.when((i == n - 1) & (i >= 1))
    def _():
        wait_slot(1 - slot)


def moe_dispatch(pos, x2, shift, scale, n_rows, *, seq, tm=512):
    t_tok, d = x2.shape
    tm = min(tm, seq)
    nt = seq // tm
    mod_spec = pl.BlockSpec((1, 1, d), lambda i, p: (i // nt, 0, 0))
    return pl.pallas_call(
        functools.partial(_dispatch_kernel, tm=tm),
        grid_spec=pltpu.PrefetchScalarGridSpec(
            num_scalar_prefetch=1,
            grid=(t_tok // tm,),
            in_specs=[pl.BlockSpec((tm, d), lambda i, p: (i, 0)), mod_spec, mod_spec,
                      pl.BlockSpec(memory_space=pl.ANY)],
            out_specs=pl.BlockSpec(memory_space=pl.ANY),
            scratch_shapes=[pltpu.VMEM((2, tm, d), F32), pltpu.SemaphoreType.DMA((2,))]),
        out_shape=jax.ShapeDtypeStruct((n_rows, d), F32),
        input_output_aliases={4: 0},
        compiler_params=_cparams("arbitrary"),
        name="moe_dispatch",
    )(pos, x2, shift, scale, jnp.zeros((n_rows, d), F32))


def _expert_kernel(te_ref, na_ref, xs_ref, w1_ref, w3_ref, w2_ref, o_ref, xb_ref):
    t = pl.program_id(0)
    k = pl.program_id(1)
    active = t < na_ref[0]

    @pl.when(active)
    def _():
        @pl.when(k == 0)
        def _():
            xb_ref[...] = xs_ref[...].astype(BF16)

        xb = xb_ref[...]
        act = _silu(_bdot(xb, w1_ref[0])) * _bdot(xb, w3_ref[0])
        y = _bdot(act.astype(BF16), w2_ref[0])

        @pl.when(k == 0)
        def _():
            o_ref[...] = y

        @pl.when(k > 0)
        def _():
            o_ref[...] += y

    @pl.when(jnp.logical_not(active) & (k == 0))
    def _():
        o_ref[...] = jnp.zeros_like(o_ref)


def moe_experts(tile_expert, n_active, xs, w1, w3, w2, *, tg, tk=512):
    n_rows, d = xs.shape
    dff = w1.shape[-1]
    nk = dff // tk
    kk = lambda t, k, na: jnp.where(t < na[0], k, nk - 1)
    return pl.pallas_call(
        _expert_kernel,
        grid_spec=pltpu.PrefetchScalarGridSpec(
            num_scalar_prefetch=2,
            grid=(n_rows // tg, nk),
            in_specs=[pl.BlockSpec((tg, d), lambda t, k, te, na: (t, 0)),
                      pl.BlockSpec((1, d, tk), lambda t, k, te, na: (te[t], 0, kk(t, k, na))),
                      pl.BlockSpec((1, d, tk), lambda t, k, te, na: (te[t], 0, kk(t, k, na))),
                      pl.BlockSpec((1, tk, d), lambda t, k, te, na: (te[t], kk(t, k, na), 0))],
            out_specs=pl.BlockSpec((tg, d), lambda t, k, te, na: (t, 0)),
            scratch_shapes=[pltpu.VMEM((tg, d), BF16)]),
        out_shape=jax.ShapeDtypeStruct((n_rows, d), F32),
        compiler_params=_cparams("arbitrary", "arbitrary"),
        name="moe_experts",
    )(tile_expert, n_active, xs, w1, w3, w2)


def _combine_kernel(pos_ref, x_ref, gt_ref, rt_ref, lg_ref, lb_ref, ys_ref, o_ref, ybuf, sem, *, tm, alpha):
    i = pl.program_id(0)
    n = pl.num_programs(0)
    slot = i % 2

    def issue(step, s):
        def body(r, carry):
            base = (step * tm + r) * TOP_K
            for k in range(TOP_K):
                pltpu.make_async_copy(ys_ref.at[pl.ds(pos_ref[base + k], 1)], ybuf.at[s, k, pl.ds(r, 1)],
                                      sem.at[s]).start()
            return carry
        lax.fori_loop(0, tm, body, 0, unroll=8)

    @pl.when(i == 0)
    def _():
        issue(0, 0)

    @pl.when(i + 1 < n)
    def _():
        issue(i + 1, 1 - slot)

    for k in range(TOP_K):
        pltpu.make_async_copy(ys_ref.at[pl.ds(0, tm)], ybuf.at[slot, k], sem.at[slot]).wait()
    rt = rt_ref[...]
    y = rt[:, 2:3] * ybuf[slot, 0] + rt[:, 3:4] * ybuf[slot, 1]
    o_ref[...] = _ln(alpha * x_ref[...] + gt_ref[0] * y) * lg_ref[...] + lb_ref[...]


def moe_combine(pos, x2, gate, route, ln_g, ln_b, ys, *, seq, alpha, tm=512):
    t_tok, d = x2.shape
    tm = min(tm, seq)
    nt = seq // tm
    vec = pl.BlockSpec((1, d), lambda i, p: (0, 0))
    return pl.pallas_call(
        functools.partial(_combine_kernel, tm=tm, alpha=alpha),
        grid_spec=pltpu.PrefetchScalarGridSpec(
            num_scalar_prefetch=1,
            grid=(t_tok // tm,),
            in_specs=[pl.BlockSpec((tm, d), lambda i, p: (i, 0)),
                      pl.BlockSpec((1, 1, d), lambda i, p: (i // nt, 0, 0)),
                      pl.BlockSpec((tm, ROUTE_COLS), lambda i, p: (i, 0)), vec, vec,
                      pl.BlockSpec(memory_space=pl.ANY)],
            out_specs=pl.BlockSpec((tm, d), lambda i, p: (i, 0)),
            scratch_shapes=[pltpu.VMEM((2, TOP_K, tm, d), F32), pltpu.SemaphoreType.DMA((2,))]),
        out_shape=jax.ShapeDtypeStruct((t_tok, d), F32),
        compiler_params=_cparams("arbitrary"),
        name="moe_combine",
    )(pos, x2, gate, route, ln_g, ln_b, ys)


def _dispatch_plan(route, *, tg):
    ids = route[:, 0:TOP_K].astype(jnp.int32).reshape(-1)
    n_assign = ids.shape[0]
    onehot = (ids[:, None] == jnp.arange(N_EXPERTS, dtype=jnp.int32)[None, :]).astype(jnp.int32)
    csum = jnp.cumsum(onehot, axis=0)
    counts = csum[-1]
    padded = (counts + tg - 1) // tg * tg
    ends = jnp.cumsum(padded)
    starts = ends - padded
    pos = jnp.sum(onehot * (starts[None, :] + csum - 1), axis=1).astype(jnp.int32)
    n_tiles = n_assign // tg + N_EXPERTS
    tile_start = jnp.arange(n_tiles, dtype=jnp.int32) * tg
    tile_expert = jnp.minimum(jnp.sum(tile_start[:, None] >= ends[None, :], axis=1), N_EXPERTS - 1)
    n_active = (ends[-1] // tg).reshape(1)
    return pos, tile_expert.astype(jnp.int32), n_active.astype(jnp.int32), n_tiles * tg


def ffn_moe(x2, shift, scale, gate, router, w1, w3, w2, ln_g, ln_b, *, seq, alpha, tg=1024):
    route = moe_route(x2, shift, scale, router, seq=seq)
    pos, tile_expert, n_active, n_rows = _dispatch_plan(route, tg=tg)
    xs = moe_dispatch(pos, x2, shift, scale, n_rows, seq=seq)
    ys = moe_experts(tile_expert, n_active, xs, w1, w3, w2, tg=tg)
    return moe_combine(pos, x2, gate, route, ln_g, ln_b, ys, seq=seq, alpha=alpha)


def _dft_matrices(seq):
    n = 2 * seq
    f = lax.broadcasted_iota(jnp.int32, (seq, seq), 0)
    t = lax.broadcasted_iota(jnp.int32, (seq, seq), 1)
    ang = ((f * t) % n).astype(F32) * (2.0 * math.pi / n)
    return jnp.cos(ang), jnp.sin(ang)


def _twiddles(seq, width):
    ang = jnp.arange(seq // 2, dtype=F32) * (math.pi / seq)
    return jnp.broadcast_to(jnp.stack([jnp.cos(ang), jnp.sin(ang)])[:, :, None], (2, seq // 2, width))


def _split_spectrum(kr, ki, kn):
    seq = kr.shape[1]
    m = seq // 2
    mirror = lambda k: k[:, seq - 1:m:-1]
    k2r = jnp.concatenate([kn, mirror(kr)], axis=1)
    k2i = jnp.concatenate([jnp.zeros_like(kn), mirror(ki)], axis=1)
    km = jnp.concatenate([kr[:, m:m + 1], ki[:, m:m + 1]], axis=1)
    return kr[:, :m], ki[:, :m], k2r, k2i, km


def _position_features(seq, n_bands, width):
    t = jnp.arange(seq, dtype=F32)[:, None]
    tn = t / max(seq - 1, 1)
    bands = jnp.arange(1, n_bands + 1, dtype=F32)
    ang = t * bands * (2.0 * math.pi / seq)
    z = jnp.concatenate([tn, jnp.cos(ang), jnp.sin(ang)], axis=-1)
    return jnp.pad(z, ((0, 0), (0, width - z.shape[1])))


def _s5_discretize(a_re, a_im, log_dt, b_re, b_im):
    lam_re = jnp.minimum(a_re, S5_MAX_RE)
    lam_im = a_im
    dt = jnp.exp(log_dt)[..., None]
    mag = jnp.exp(lam_re * dt)
    ang = lam_im * dt
    abar_re = mag * jnp.cos(ang)
    abar_im = mag * jnp.sin(ang)
    den = lam_re * lam_re + lam_im * lam_im
    q_re = ((abar_re - 1.0) * lam_re + abar_im * lam_im) / den
    q_im = (abar_im * lam_re - (abar_re - 1.0) * lam_im) / den
    bb_re = q_re[..., None] * b_re - q_im[..., None] * b_im
    bb_im = q_re[..., None] * b_im + q_im[..., None] * b_re
    return abar_re, abar_im, bb_re, bb_im


def _s5_matrices(a_re, a_im, log_dt, b_re, b_im, c_re, c_im):
    abr, abi, bbr, bbi = _s5_discretize(a_re, a_im, log_dt, b_re, b_im)
    g_all, p, k = bbr.shape[1:]
    gb = S5_BLOCK_CH // S5_GROUP
    nj = g_all // gb
    eye = jnp.eye(gb, dtype=F32)

    def in_mat(bb):
        t = jnp.swapaxes(bb, -1, -2).reshape(2, nj, gb, k, p)
        return jnp.einsum("djgkp,gh->djgkhp", t, eye).reshape(2, nj, gb * k, gb * p)

    def out_mat(cc):
        t = jnp.swapaxes(cc, -1, -2).reshape(2, nj, gb, p, k)
        return jnp.einsum("djhpc,hg->djhpgc", t, eye).reshape(2, nj, gb * p, gb * k)

    bmat = jnp.concatenate([in_mat(bbr), in_mat(bbi)], axis=-1).astype(BF16)
    cmat = jnp.concatenate([out_mat(c_re), -out_mat(c_im)], axis=-2).astype(BF16)
    abar = jnp.stack([abr.reshape(2, nj, gb * p), abi.reshape(2, nj, gb * p)], axis=2)
    return bmat, cmat, abar


def kernel(x, c, ctx, c_ctx, w_mod, b_mod, w_in, hy_short_w, hy_short_b, hy_f_w1, hy_f_b1, hy_f_w2, hy_f_b2, hy_f_w3, hy_f_b3, hy_freq, hy_decay, hy_bias, w_hy_out, cf_dw_w, cf_dw_b, cf_ln_g, cf_ln_b, w_cf_out, s5_a_re, s5_a_im, s5_log_dt, s5_b_re, s5_b_im, s5_c_re, s5_c_im, s5_d, s5_w_glu, s5_b_glu, w_s5_out, w_o, ln1_g, ln1_b, ln2_g, ln2_b, ffn_w1, ffn_w3, ffn_w2, moe_router, moe_w1, moe_w3, moe_w2):
    bsz, seq, d = x.shape
    seq_c = ctx.shape[1]
    depth = w_mod.shape[0]
    alpha = (2.0 * depth) ** 0.25
    ch = D_BRANCH
    hy_cols = (HY_ORDER + 1) * ch
    cf_off = hy_cols
    s5_off = cf_off + 2 * ch
    gate_off = s5_off + ch
    n_bands = (hy_f_w1.shape[1] - 1) // 2
    nj = ch // S5_BLOCK_CH

    x2 = x.reshape(bsz * seq, d)
    xc2 = ctx.reshape(bsz * seq_c, d)
    mod_rows = 24
    cc = jnp.zeros((mod_rows, d), F32).at[:bsz].set(c).at[bsz].set(c_ctx)

    dft = {}
    feats = {}
    for s in (seq, seq_c):
        cs, ss = _dft_matrices(s)
        csh, ssh = _dft_matrices(s // 2)
        dft[s] = (cs, ss, csh.astype(BF16), ssh.astype(BF16), _twiddles(s, ch))
        feats[s] = _position_features(s, n_bands, 128)
    zero_state = jnp.zeros((2, nj, bsz, 2 * S5_BLOCK_ST), F32)
    row = lambda v: v[None, :]

    for l in range(depth):
        last = l == depth - 1
        mods = mod_vectors(cc, w_mod[l].astype(BF16), row(b_mod[l])).reshape(mod_rows, 6, d)
        lat = [mods[:bsz, i][:, None, :] for i in range(6)]
        cxm = [jnp.broadcast_to(mods[bsz, i][None, None, :], (bsz, 1, d)) for i in range(6)]

        w_in_b = w_in[l].astype(BF16)
        w_mix = w_in_b[:, :gate_off]
        w_s5 = w_in_b[:, s5_off:gate_off]
        w_gate = w_in_b[:, gate_off:]
        sw = jnp.concatenate([hy_short_w[l][:, i * ch:(i + 1) * ch] for i in range(HY_ORDER + 1)], axis=0)
        sb = hy_short_b[l].reshape(HY_ORDER + 1, ch)
        w1p = jnp.pad(hy_f_w1[l], ((0, 128 - hy_f_w1.shape[1]), (0, 0)))
        bmat, cmat, abar = _s5_matrices(s5_a_re[l], s5_a_im[l], s5_log_dt[l], s5_b_re[l], s5_b_im[l],
                                        s5_c_re[l], s5_c_im[l])
        merge_w = (w_gate, w_hy_out[l].astype(BF16), w_cf_out[l].astype(BF16), row(s5_d[l]),
                   s5_w_glu[l].astype(BF16), row(s5_b_glu[l]), w_s5_out[l].astype(BF16),
                   w_o[l].astype(BF16), row(ln1_g[l]), row(ln1_b[l]))

        def spectra(s):
            filt = hyena_filters(feats[s], w1p, row(hy_f_b1[l]), hy_f_w2[l], row(hy_f_b2[l]), hy_f_w3[l],
                                 row(hy_f_b3[l]), hy_freq[l], row(hy_decay[l]))
            return _split_spectrum(*hyena_spectra(dft[s][0], dft[s][1], filt))

        def token_mixer(xs, s, seg, m, init):
            sh1, sc1, g1 = m[0], m[1], m[2]
            p = ln_mod_proj(xs, sh1, sc1, w_mix, seq=s)
            z_hy = hyena_mix(p, sw, sb, dft[s][2], dft[s][3], dft[s][4], *spectra(s), hy_bias[l], seq=s)
            z_cf = conformer_mix(p, cf_dw_w[l], row(cf_dw_b[l]), row(cf_ln_g[l]), row(cf_ln_b[l]), seg=seg,
                                 col_block=cf_off // ch)
            ys, fin = s5_scan(p, bmat, cmat, abar, init, seq=s, col_off=s5_off)
            x1 = merge_branches(xs, sh1, sc1, g1, z_hy, z_cf, ys, p, *merge_w, seq=s, alpha=alpha,
                                s5_col_block=s5_off // ch)
            return x1, fin

        def channel_mixer(xs, s, m):
            sh2, sc2, g2 = m[3], m[4], m[5]
            i = l // 2
            if l % 2 == 0:
                return ffn_dense(xs, sh2, sc2, g2, ffn_w1[i].astype(BF16), ffn_w3[i].astype(BF16),
                                 ffn_w2[i].astype(BF16), row(ln2_g[l]), row(ln2_b[l]), seq=s, alpha=alpha)
            router = jnp.pad(moe_router[i], ((0, 0), (0, ROUTER_LANES - moe_router.shape[-1])))
            return ffn_moe(xs, sh2, sc2, g2, router, moe_w1[i].astype(BF16), moe_w3[i].astype(BF16),
                           moe_w2[i].astype(BF16), row(ln2_g[l]), row(ln2_b[l]), seq=s, alpha=alpha)

        if last:
            u_s5 = ln_mod_proj(xc2, cxm[0], cxm[1], w_s5, seq=seq_c)
            _, finals = s5_scan(u_s5, bmat, cmat, abar, zero_state, seq=seq_c, col_off=0)
        else:
            xc2, finals = token_mixer(xc2, seq_c, seq_c, cxm, zero_state)
            xc2 = channel_mixer(xc2, seq_c, cxm)

        x2, _ = token_mixer(x2, seq, GRID_W, lat, finals)
        x2 = channel_mixer(x2, seq, lat)
    return x2.reshape(bsz, seq, d)
```

```python
import functools
import math

import jax
import jax.numpy as jnp
from jax import lax
from jax.experimental import pallas as pl
from jax.experimental.pallas import tpu as pltpu

F32 = jnp.float32
BF16 = jnp.bfloat16
HIGHEST = lax.Precision.HIGHEST

V7X_VMEM_LIMIT_BYTES = 56 * 1024 * 1024

LN_EPS = 1e-5
D_BRANCH = 512
HY_ORDER = 2
HY_SHORT = 3
HY_ROW_BLOCK = 512
CF_KERNEL = 31
SUBLANES = 8
LANES = 128
CF_PAD = 16
CF_SHIFT_SPAN = 24
GRID_W = 64
S5_GROUP = 16
S5_STATE = 64
S5_MAX_RE = -1e-4
S5_TIME_BLOCK = 16
S5_SUB_CHUNK = 32
S5_BLOCK_CH = 128
S5_BLOCK_ST = S5_BLOCK_CH // S5_GROUP * S5_STATE
N_EXPERTS = 8
TOP_K = 2
ROUTER_LANES = 128
ROUTE_COLS = 8


def _cparams(*sem):
    return pltpu.CompilerParams(dimension_semantics=sem, vmem_limit_bytes=V7X_VMEM_LIMIT_BYTES)


def _const_spec(shape):
    nd = len(shape)
    return pl.BlockSpec(shape, lambda *_: (0,) * nd, pipeline_mode=pl.Buffered(1))


def _bdot(a, b):
    return jnp.dot(a, b, preferred_element_type=F32)


def _sigmoid(x):
    return 1.0 / (1.0 + jnp.exp(-x))


def _silu(x):
    return x * _sigmoid(x)


def _gelu_tanh(x):
    return 0.5 * x * (1.0 + jnp.tanh(math.sqrt(2.0 / math.pi) * (x + 0.044715 * (x * x * x))))


def _ln(x):
    mu = jnp.mean(x, axis=-1, keepdims=True)
    xc = x - mu
    var = jnp.mean(xc * xc, axis=-1, keepdims=True)
    return xc * lax.rsqrt(var + LN_EPS)


def _ln_mod(x, shift, scale):
    return _ln(x) * (1.0 + scale) + shift


def _token_tiling(t_tok, seq, cap, shared_mod):
    if shared_mod:
        return min(t_tok, cap), lambda i: 0
    tm = min(seq, cap)
    nt = seq // tm
    return tm, lambda i: i // nt


def _mod_kernel(c_ref, w_ref, b_ref, o_ref):
    s = _silu(c_ref[...])
    o_ref[...] = _bdot(s.astype(BF16), w_ref[...]) + b_ref[...]


def mod_vectors(cc, w, b, *, tn=1536):
    r, d = cc.shape
    n = w.shape[1]
    return pl.pallas_call(
        _mod_kernel,
        grid=(n // tn,),
        in_specs=[pl.BlockSpec((r, d), lambda j: (0, 0)),
                  pl.BlockSpec((d, tn), lambda j: (0, j)),
                  pl.BlockSpec((1, tn), lambda j: (0, j))],
        out_specs=pl.BlockSpec((r, tn), lambda j: (0, j)),
        out_shape=jax.ShapeDtypeStruct((r, n), F32),
        compiler_params=_cparams("arbitrary"),
        name="mod_vectors",
    )(cc, w, b)


def _proj_kernel(x_ref, sh_ref, sc_ref, w_ref, o_ref, *, tn):
    h = _ln_mod(x_ref[...], sh_ref[0], sc_ref[0]).astype(BF16)
    for j in range(o_ref.shape[1] // tn):
        cols = slice(j * tn, (j + 1) * tn)
        o_ref[:, cols] = _bdot(h, w_ref[:, cols])


def ln_mod_proj(x2, shift, scale, w, *, seq, shared_mod=False, tn=512):
    t_tok, d = x2.shape
    n = w.shape[1]
    tm, brow = _token_tiling(t_tok, seq, 1024, shared_mod)
    mod_spec = pl.BlockSpec((1, 1, d), lambda i: (brow(i), 0, 0))
    return pl.pallas_call(
        functools.partial(_proj_kernel, tn=tn),
        grid=(t_tok // tm,),
        in_specs=[pl.BlockSpec((tm, d), lambda i: (i, 0)), mod_spec, mod_spec, _const_spec((d, n))],
        out_specs=pl.BlockSpec((tm, n), lambda i: (i, 0)),
        out_shape=jax.ShapeDtypeStruct((t_tok, n), F32),
        compiler_params=_cparams("arbitrary"),
        name="ln_mod_proj",
    )(x2, shift, scale, w)


def _filter_kernel(z_ref, w1_ref, b1_ref, w2_ref, b2_ref, w3_ref, b3_ref, fr_ref, dec_ref, o_ref):
    z = z_ref[...]
    h = jnp.sin(fr_ref[0:1, :] * (jnp.dot(z, w1_ref[...], precision=HIGHEST, preferred_element_type=F32)
                                  + b1_ref[...]))
    h = jnp.sin(fr_ref[1:2, :] * (jnp.dot(h, w2_ref[...], precision=HIGHEST, preferred_element_type=F32)
                                  + b2_ref[...]))
    h = jnp.dot(h, w3_ref[...], precision=HIGHEST, preferred_element_type=F32) + b3_ref[...]
    tn = z[:, 0:1]
    o_ref[...] = h * jnp.exp(-tn * jnp.abs(dec_ref[...]))


def hyena_filters(z, w1, b1, w2, b2, w3, b3, freq, decay, *, tn=512):
    seq, zf = z.shape
    hid = w2.shape[0]
    n = w3.shape[1]
    full = lambda shape: pl.BlockSpec(shape, lambda j: (0, 0))
    return pl.pallas_call(
        _filter_kernel,
        grid=(n // tn,),
        in_specs=[full((seq, zf)), full((zf, hid)), full((1, hid)), full((hid, hid)), full((1, hid)),
                  pl.BlockSpec((hid, tn), lambda j: (0, j)), pl.BlockSpec((1, tn), lambda j: (0, j)),
                  full((2, hid)), pl.BlockSpec((1, tn), lambda j: (0, j))],
        out_specs=pl.BlockSpec((seq, tn), lambda j: (0, j)),
        out_shape=jax.ShapeDtypeStruct((seq, n), F32),
        compiler_params=_cparams("arbitrary"),
        name="hyena_filters",
    )(z, w1, b1, w2, b2, w3, b3, freq, decay)


def _spectrum_kernel(cs_ref, ss_ref, filt_ref, k1r_ref, k1i_ref, k2r_ref, k2i_ref, km_ref, *, tf, n_fft):
    ch = k1r_ref.shape[-1]
    seq = filt_ref.shape[0]
    hf = filt_ref[:, 0:ch]
    tau = lax.broadcasted_iota(jnp.int32, (seq, 1), 0)
    hb = jnp.where(tau == 0, 0.0, filt_ref[:, ch:2 * ch])
    s_cos = hf + hb
    s_sin = hb - hf
    sign = (1 - 2 * (tau & 1)).astype(F32)
    f = pl.program_id(1) * tf + lax.broadcasted_iota(jnp.int32, (tf, 1), 0)
    wgt = jnp.where(f == 0, 1.0, 2.0) / n_fft
    hdot = lambda a, b: jnp.dot(a, b, precision=HIGHEST, preferred_element_type=F32)
    cs = cs_ref[...]
    ss = ss_ref[...]
    k1r_ref[0] = wgt * hdot(cs, s_cos)
    k1i_ref[0] = wgt * hdot(ss, s_sin)
    k2r_ref[0] = wgt * hdot(cs, sign * s_cos)
    k2i_ref[0] = -wgt * hdot(ss, sign * s_sin)
    quarter = tau & 3
    c4 = jnp.where(quarter == 0, 1.0, jnp.where(quarter == 2, -1.0, 0.0))
    s4 = jnp.where(quarter == 1, 1.0, jnp.where(quarter == 3, -1.0, 0.0))
    km_ref[0, 0:1, :] = jnp.sum(c4 * s_cos, axis=0, keepdims=True) * (2.0 / n_fft)
    km_ref[0, 1:2, :] = jnp.sum(s4 * s_sin, axis=0, keepdims=True) * (2.0 / n_fft)


def hyena_spectra(cs, ss, filt, *, tf=256):
    m, seq = cs.shape
    ch = D_BRANCH
    tf = min(tf, m)
    rows = pl.BlockSpec((tf, seq), lambda o, f: (f, 0))
    out = pl.BlockSpec((1, tf, ch), lambda o, f: (o, f, 0))
    return pl.pallas_call(
        functools.partial(_spectrum_kernel, tf=tf, n_fft=2 * seq),
        grid=(HY_ORDER, m // tf),
        in_specs=[rows, rows, pl.BlockSpec((seq, 2 * ch), lambda o, f: (0, o))],
        out_specs=[out] * 4 + [pl.BlockSpec((1, 2, ch), lambda o, f: (o, 0, 0))],
        out_shape=[jax.ShapeDtypeStruct((HY_ORDER, m, ch), F32)] * 4
                  + [jax.ShapeDtypeStruct((HY_ORDER, 2, ch), F32)],
        compiler_params=_cparams("arbitrary", "arbitrary"),
        name="hyena_spectra",
    )(cs, ss, filt)


def _hyena_kernel(*refs, nh):
    parts = [refs[p * nh:(p + 1) * nh] for p in range(HY_ORDER + 1)]
    (sw_ref, sb_ref, cs_ref, ss_ref, tw_ref, k1r_ref, k1i_ref, k2r_ref, k2i_ref, km_ref, bias_ref,
     o_ref, ze_ref, zo_ref, xe_ref, xo_ref, zeb_ref, zob_ref, g_ref, il_ref) = refs[(HY_ORDER + 1) * nh:]
    m = o_ref.shape[0] // 2
    row = lax.broadcasted_iota(jnp.int32, (m, 1), 0)
    sign = (1 - 2 * (row & 1)).astype(F32)

    def short_conv(part):
        pe = jnp.concatenate([r[pl.ds(0, m, stride=2), :] for r in parts[part]], axis=1)
        po = jnp.concatenate([r[pl.ds(1, m, stride=2), :] for r in parts[part]], axis=1)
        po_prev = jnp.where(row == 0, 0.0, pltpu.roll(po, 1, 0))
        pe_next = jnp.where(row == m - 1, 0.0, pltpu.roll(pe, m - 1, 0))
        w = sw_ref[part * HY_SHORT:(part + 1) * HY_SHORT, :]
        b = sb_ref[part:part + 1, :]
        return (w[0:1] * po_prev + w[1:2] * pe + w[2:3] * po + b,
                w[0:1] * pe + w[1:2] * po + w[2:3] * pe_next + b)

    rb = min(m, HY_ROW_BLOCK)
    blocks = [slice(r * rb, (r + 1) * rb) for r in range(m // rb)]
    ze_ref[...], zo_ref[...] = short_conv(HY_ORDER)
    for o in range(HY_ORDER):
        ze = ze_ref[...]
        zo = zo_ref[...]
        zeb_ref[...] = ze.astype(BF16)
        zob_ref[...] = zo.astype(BF16)
        e_m = jnp.sum(sign * ze, axis=0, keepdims=True)
        o_m = jnp.sum(sign * zo, axis=0, keepdims=True)
        kmr = km_ref[o, 0:1, :]
        kmi = km_ref[o, 1:2, :]
        nyq_e = e_m * kmr + o_m * kmi
        nyq_o = o_m * kmr - e_m * kmi
        bias = bias_ref[o:o + 1, :]
        xe_ref[...], xo_ref[...] = short_conv(o)
        for rows in blocks:
            cs = cs_ref[rows, :]
            ss = ss_ref[rows, :]
            ae = _bdot(cs, zeb_ref[...])
            be = _bdot(ss, zeb_ref[...])
            ao = _bdot(cs, zob_ref[...])
            bo = _bdot(ss, zob_ref[...])
            c = tw_ref[0, rows, :]
            s = tw_ref[1, rows, :]
            tr = c * ao - s * bo
            ti = -(c * bo + s * ao)
            pr, pi = ae + tr, ti - be
            qr, qi = ae - tr, -be - ti
            k1r, k1i = k1r_ref[o, rows, :], k1i_ref[o, rows, :]
            k2r, k2i = k2r_ref[o, rows, :], k2i_ref[o, rows, :]
            z1r = pr * k1r - pi * k1i
            z1i = pr * k1i + pi * k1r
            z2r = qr * k2r + qi * k2i
            z2i = qr * k2i - qi * k2r
            dr, di = z1r - z2r, z1i + z2i
            g_ref[0, rows, :] = (z1r + z2r).astype(BF16)
            g_ref[1, rows, :] = (z1i - z2i).astype(BF16)
            g_ref[2, rows, :] = (c * dr - s * di).astype(BF16)
            g_ref[3, rows, :] = (c * di + s * dr).astype(BF16)
        for rows in blocks:
            cs = cs_ref[rows, :]
            ss = ss_ref[rows, :]
            ye = _bdot(cs, g_ref[0]) - _bdot(ss, g_ref[1]) + sign[rows] * nyq_e + bias * ze_ref[rows, :]
            yo = _bdot(cs, g_ref[2]) - _bdot(ss, g_ref[3]) + sign[rows] * nyq_o + bias * zo_ref[rows, :]
            ze_ref[rows, :] = xe_ref[rows, :] * ye
            zo_ref[rows, :] = xo_ref[rows, :] * yo
    for h in range(nh):
        lanes = slice(h * LANES, (h + 1) * LANES)
        il_ref[pl.ds(0, m, stride=2), :] = ze_ref[:, lanes]
        il_ref[pl.ds(1, m, stride=2), :] = zo_ref[:, lanes]
        o_ref[:, lanes] = il_ref[...]


def hyena_mix(p, sw, sb, cs, ss, tw, k1r, k1i, k2r, k2i, km, bias, *, seq, cb=256):
    t_tok = p.shape[0]
    ch = D_BRANCH
    bsz = t_tok // seq
    m = seq // 2
    nh = cb // LANES
    col_blocks = ch // LANES
    cols = [pl.BlockSpec((seq, LANES), functools.partial(lambda j, b, c0: (b, c0 + j * nh), c0=part * col_blocks + h))
            for part in range(HY_ORDER + 1) for h in range(nh)]
    per_c = lambda rows: pl.BlockSpec((rows, cb), lambda j, b: (0, j))
    spec3 = lambda lead, rows: pl.BlockSpec((lead, rows, cb), lambda j, b: (0, 0, j),
                                            pipeline_mode=pl.Buffered(1))
    return pl.pallas_call(
        functools.partial(_hyena_kernel, nh=nh),
        grid=(ch // cb, bsz),
        in_specs=cols + [per_c(3 * HY_SHORT), per_c(3), _const_spec((m, m)), _const_spec((m, m)),
                         spec3(2, m)] + [spec3(HY_ORDER, m)] * 4 + [spec3(HY_ORDER, 2), per_c(HY_ORDER)],
        out_specs=pl.BlockSpec((seq, cb), lambda j, b: (b, j)),
        out_shape=jax.ShapeDtypeStruct((t_tok, ch), F32),
        scratch_shapes=[pltpu.VMEM((m, cb), F32)] * 4 + [pltpu.VMEM((m, cb), BF16)] * 2
                       + [pltpu.VMEM((4, m, cb), BF16), pltpu.VMEM((seq, LANES), F32)],
        compiler_params=_cparams("arbitrary", "arbitrary"),
        name="hyena_mix",
    )(*([p] * ((HY_ORDER + 1) * nh)), sw, sb, cs, ss, tw, k1r, k1i, k2r, k2i, km, bias)


def _conformer_kernel(a_ref, gl_ref, w_ref, b_ref, g_ref, be_ref, o_ref, pad_ref, shift_ref, *, seg, nseg):
    ch = a_ref.shape[-1]
    zeros = jnp.zeros((nseg, CF_PAD, ch), F32)
    pad_ref[:, 0:CF_PAD, :] = zeros
    pad_ref[:, CF_PAD + seg:, :] = zeros
    u = a_ref[...] * _sigmoid(gl_ref[...])
    pad_ref[:, CF_PAD:CF_PAD + seg, :] = u.reshape(nseg, seg, ch)
    span = seg + CF_SHIFT_SPAN
    for r in range(SUBLANES):
        shift_ref[r] = pad_ref[:, r:r + span, :]
    half = (CF_KERNEL - 1) // 2
    for c in range(ch // LANES):
        lanes = slice(c * LANES, (c + 1) * LANES)
        for s in range(nseg):
            acc = jnp.zeros((seg, LANES), F32)
            for k in range(CF_KERNEL):
                q, r = divmod(CF_PAD - half + k, SUBLANES)
                acc = acc + w_ref[k:k + 1, lanes] * shift_ref[r, s, q * SUBLANES:q * SUBLANES + seg, lanes]
            o_ref[s * seg:(s + 1) * seg, lanes] = acc
    y = _ln(o_ref[...] + b_ref[...]) * g_ref[...] + be_ref[...]
    o_ref[...] = _silu(y)


def conformer_mix(p, dw_w, dw_b, ln_g, ln_b, *, seg, col_block):
    t_tok = p.shape[0]
    ch = D_BRANCH
    tm = max(seg, 512)
    nseg = tm // seg
    vec = lambda rows: pl.BlockSpec((rows, ch), lambda i: (0, 0))
    return pl.pallas_call(
        functools.partial(_conformer_kernel, seg=seg, nseg=nseg),
        grid=(t_tok // tm,),
        in_specs=[pl.BlockSpec((tm, ch), lambda i: (i, col_block)),
                  pl.BlockSpec((tm, ch), lambda i: (i, col_block + 1)),
                  vec(CF_KERNEL), vec(1), vec(1), vec(1)],
        out_specs=pl.BlockSpec((tm, ch), lambda i: (i, 0)),
        out_shape=jax.ShapeDtypeStruct((t_tok, ch), F32),
        scratch_shapes=[pltpu.VMEM((nseg, seg + 2 * CF_PAD, ch), F32),
                        pltpu.VMEM((SUBLANES, nseg, seg + CF_SHIFT_SPAN, ch), F32)],
        compiler_params=_cparams("arbitrary"),
        name="conformer_mix",
    )(p, p, dw_w, dw_b, ln_g, ln_b)


def _s5_kernel(u_ref, b_ref, c_ref, a_ref, init_ref, y_ref, fin_ref, u1_ref, utm_ref, bu_ref, sb_ref,
               ytm_ref, st_ref, *, tc, bsz, reverse):
    c = pl.program_id(1)
    ns = S5_BLOCK_ST

    @pl.when(c == 0)
    def _():
        st_ref[...] = init_ref[0]

    tb = S5_TIME_BLOCK
    for th in range(tc // tb):
        for b in range(bsz):
            u1_ref[pl.ds((th * bsz + b) * tb, tb), :] = u_ref[b, th * tb:(th + 1) * tb, :]
    for th in range(tc // tb):
        for tl in range(tb):
            utm_ref[pl.ds((th * tb + tl) * bsz, bsz), :] = u1_ref[pl.ds(th * bsz * tb + tl, bsz, stride=tb), :]
    a_re = jnp.broadcast_to(a_ref[0, 0:1, :], (bsz, ns))
    a_im = jnp.broadcast_to(a_ref[0, 1:2, :], (bsz, ns))
    s_re = st_ref[:, 0:ns]
    s_im = st_ref[:, ns:2 * ns]
    sub = min(tc, S5_SUB_CHUNK)
    order = (lambda n: range(n - 1, -1, -1)) if reverse else range
    for sc in order(tc // sub):
        r0 = sc * sub * bsz
        rows = slice(r0, r0 + sub * bsz)
        bu_ref[rows, :] = _bdot(utm_ref[rows, :].astype(BF16), b_ref[0])
        for tl in order(sub):
            rr = slice(r0 + tl * bsz, r0 + (tl + 1) * bsz)
            n_re = a_re * s_re - a_im * s_im + bu_ref[rr, 0:ns]
            n_im = a_re * s_im + a_im * s_re + bu_ref[rr, ns:2 * ns]
            sb_ref[rr, 0:ns] = n_re.astype(BF16)
            sb_ref[rr, ns:2 * ns] = n_im.astype(BF16)
            s_re, s_im = n_re, n_im
        ytm_ref[rows, :] = _bdot(sb_ref[rows, :], c_ref[0])
    st_ref[:, 0:ns] = s_re
    st_ref[:, ns:2 * ns] = s_im
    for b in range(bsz):
        y_ref[b] = ytm_ref[pl.ds(b, tc, stride=bsz), :]

    @pl.when(c == pl.num_programs(1) - 1)
    def _():
        fin_ref[0] = st_ref[...]


def s5_scan(p, bmat, cmat, abar, init, *, seq, col_off, tc=128):
    t_tok, n_cols = p.shape
    ch = D_BRANCH
    bsz = t_tok // seq
    u3 = p.reshape(bsz, seq, n_cols)
    j0 = col_off // S5_BLOCK_CH
    nj = ch // S5_BLOCK_CH
    tc = min(tc, seq)
    nc = seq // tc
    rows = tc * bsz
    ns2 = 2 * S5_BLOCK_ST
    par = lambda r, cdim: pl.BlockSpec((1, r, cdim), lambda j, c: (j, 0, 0))
    ys, fins = [], []
    for d, reverse in enumerate((False, True)):
        chunk = (lambda c: nc - 1 - c) if reverse else (lambda c: c)
        y, fin = pl.pallas_call(
            functools.partial(_s5_kernel, tc=tc, bsz=bsz, reverse=reverse),
            grid=(nj, nc),
            in_specs=[pl.BlockSpec((bsz, tc, S5_BLOCK_CH),
                                   functools.partial(lambda j, c, ck: (0, ck(c), j0 + j), ck=chunk)),
                      par(S5_BLOCK_CH, ns2), par(ns2, S5_BLOCK_CH), par(2, S5_BLOCK_ST), par(bsz, ns2)],
            out_specs=[pl.BlockSpec((bsz, tc, S5_BLOCK_CH),
                                    functools.partial(lambda j, c, ck: (0, ck(c), j), ck=chunk)),
                       par(bsz, ns2)],
            out_shape=[jax.ShapeDtypeStruct((bsz, seq, ch), F32),
                       jax.ShapeDtypeStruct((nj, bsz, ns2), F32)],
            scratch_shapes=[pltpu.VMEM((rows, S5_BLOCK_CH), F32), pltpu.VMEM((rows, S5_BLOCK_CH), F32),
                            pltpu.VMEM((rows, ns2), F32), pltpu.VMEM((rows, ns2), BF16),
                            pltpu.VMEM((rows, S5_BLOCK_CH), F32), pltpu.VMEM((bsz, ns2), F32)],
            compiler_params=_cparams("arbitrary", "arbitrary"),
            name="s5_scan_bwd" if reverse else "s5_scan_fwd",
        )(u3, bmat[d], cmat[d], abar[d], init[d])
        ys.append(y.reshape(t_tok, ch))
        fins.append(fin)
    return ys, jnp.stack(fins)


def _merge_kernel(x_ref, sh_ref, sc_ref, gt_ref, zh_ref, zc_ref, ysf_ref, ysb_ref, us_ref,
                  wg_ref, why_ref, wcf_ref, ds_ref, wglu_ref, bglu_ref, ws5_ref, wo_ref,
                  lg_ref, lb_ref, o_ref, *, alpha):
    d = x_ref.shape[-1]
    x = x_ref[...]
    h = _ln_mod(x, sh_ref[0], sc_ref[0]).astype(BF16)
    gate = lambda i: _sigmoid(_bdot(h, wg_ref[:, i * d:(i + 1) * d]))
    mix = gate(0) * _bdot(zh_ref[...].astype(BF16), why_ref[...])
    mix = mix + gate(1) * _bdot(zc_ref[...].astype(BF16), wcf_ref[...])
    ys = ysf_ref[...] + ysb_ref[...] + ds_ref[...] * us_ref[...]
    g = _gelu_tanh(ys)
    y5 = g * _sigmoid(_bdot(g.astype(BF16), wglu_ref[...]) + bglu_ref[...])
    mix = mix + gate(2) * _bdot(y5.astype(BF16), ws5_ref[...])
    y = _bdot(mix.astype(BF16), wo_ref[...])
    o_ref[...] = _ln(alpha * x + gt_ref[0] * y) * lg_ref[...] + lb_ref[...]


def merge_branches(x2, shift, scale, gate, z_hy, z_cf, ys, p, w_gate, w_hy, w_cf, d_skip,
                   w_glu, b_glu, w_s5, w_o, ln_g, ln_b, *, seq, alpha, s5_col_block, shared_mod=False):
    t_tok, d = x2.shape
    ch = D_BRANCH
    tm, brow = _token_tiling(t_tok, seq, 512, shared_mod)
    mod_spec = pl.BlockSpec((1, 1, d), lambda i: (brow(i), 0, 0))
    tok = lambda n: pl.BlockSpec((tm, n), lambda i: (i, 0))
    return pl.pallas_call(
        functools.partial(_merge_kernel, alpha=alpha),
        grid=(t_tok // tm,),
        in_specs=[tok(d), mod_spec, mod_spec, mod_spec, tok(ch), tok(ch), tok(ch), tok(ch),
                  pl.BlockSpec((tm, ch), lambda i: (i, s5_col_block)),
                  _const_spec((d, 3 * d)), _const_spec((ch, d)), _const_spec((ch, d)),
                  _const_spec((1, ch)), _const_spec((ch, ch)), _const_spec((1, ch)),
                  _const_spec((ch, d)), _const_spec((d, d)), _const_spec((1, d)), _const_spec((1, d))],
        out_specs=tok(d),
        out_shape=jax.ShapeDtypeStruct((t_tok, d), F32),
        compiler_params=_cparams("arbitrary"),
        name="merge_branches",
    )(x2, shift, scale, gate, z_hy, z_cf, ys[0], ys[1], p, w_gate, w_hy, w_cf, d_skip, w_glu, b_glu,
      w_s5, w_o, ln_g, ln_b)


def _ffn_kernel(x_ref, sh_ref, sc_ref, gt_ref, w1_ref, w3_ref, w2_ref, lg_ref, lb_ref, o_ref, acc_ref,
                *, alpha, tk):
    x = x_ref[...]
    h = _ln_mod(x, sh_ref[0], sc_ref[0]).astype(BF16)
    for k in range(w1_ref.shape[1] // tk):
        cols = slice(k * tk, (k + 1) * tk)
        act = _silu(_bdot(h, w1_ref[:, cols])) * _bdot(h, w3_ref[:, cols])
        y = _bdot(act.astype(BF16), w2_ref[cols, :])
        if k == 0:
            acc_ref[...] = y
        else:
            acc_ref[...] += y
    o_ref[...] = _ln(alpha * x + gt_ref[0] * acc_ref[...]) * lg_ref[...] + lb_ref[...]


def ffn_dense(x2, shift, scale, gate, w1, w3, w2, ln_g, ln_b, *, seq, alpha, shared_mod=False, tk=256):
    t_tok, d = x2.shape
    dff = w1.shape[1]
    tm, brow = _token_tiling(t_tok, seq, 1024, shared_mod)
    mod_spec = pl.BlockSpec((1, 1, d), lambda i: (brow(i), 0, 0))
    return pl.pallas_call(
        functools.partial(_ffn_kernel, alpha=alpha, tk=tk),
        grid=(t_tok // tm,),
        in_specs=[pl.BlockSpec((tm, d), lambda i: (i, 0)), mod_spec, mod_spec, mod_spec,
                  _const_spec((d, dff)), _const_spec((d, dff)), _const_spec((dff, d)),
                  _const_spec((1, d)), _const_spec((1, d))],
        out_specs=pl.BlockSpec((tm, d), lambda i: (i, 0)),
        out_shape=jax.ShapeDtypeStruct((t_tok, d), F32),
        scratch_shapes=[pltpu.VMEM((tm, d), F32)],
        compiler_params=_cparams("arbitrary"),
        name="ffn_dense",
    )(x2, shift, scale, gate, w1, w3, w2, ln_g, ln_b)


def _router_top2(hf, router):
    logits = jnp.dot(hf, router, precision=HIGHEST, preferred_element_type=F32)
    lane = lax.broadcasted_iota(jnp.int32, logits.shape, 1).astype(F32)
    neg = jnp.float32(-jnp.inf)
    logits = jnp.where(lane < N_EXPERTS, logits, neg)
    m1 = jnp.max(logits, axis=-1, keepdims=True)
    i1 = jnp.min(jnp.where(logits == m1, lane, float(ROUTER_LANES)), axis=-1, keepdims=True)
    rest = jnp.where(lane == i1, neg, logits)
    m2 = jnp.max(rest, axis=-1, keepdims=True)
    i2 = jnp.min(jnp.where(rest == m2, lane, float(ROUTER_LANES)), axis=-1, keepdims=True)
    e2 = jnp.exp(m2 - m1)
    return i1, i2, 1.0 / (1.0 + e2), e2 / (1.0 + e2)


def _route_kernel(x_ref, sh_ref, sc_ref, r_ref, o_ref):
    hf = _ln_mod(x_ref[...], sh_ref[0], sc_ref[0])
    i1, i2, w1, w2 = _router_top2(hf, r_ref[...])
    lane = lax.broadcasted_iota(jnp.int32, o_ref.shape, 1)
    o_ref[...] = jnp.where(lane == 0, i1, jnp.where(lane == 1, i2, jnp.where(lane == 2, w1,
                           jnp.where(lane == 3, w2, 0.0))))


def moe_route(x2, shift, scale, router, *, seq):
    t_tok, d = x2.shape
    tm = min(seq, 1024)
    nt = seq // tm
    mod_spec = pl.BlockSpec((1, 1, d), lambda i: (i // nt, 0, 0))
    return pl.pallas_call(
        _route_kernel,
        grid=(t_tok // tm,),
        in_specs=[pl.BlockSpec((tm, d), lambda i: (i, 0)), mod_spec, mod_spec,
                  pl.BlockSpec((d, ROUTER_LANES), lambda i: (0, 0))],
        out_specs=pl.BlockSpec((tm, ROUTE_COLS), lambda i: (i, 0)),
        out_shape=jax.ShapeDtypeStruct((t_tok, ROUTE_COLS), F32),
        compiler_params=_cparams("arbitrary"),
        name="moe_route",
    )(x2, shift, scale, router)


def _dispatch_kernel(pos_ref, x_ref, sh_ref, sc_ref, xs_in_ref, xs_ref, hbuf, sem, *, tm):
    del xs_in_ref
    i = pl.program_id(0)
    n = pl.num_programs(0)
    slot = i % 2

    def row_copy(s, r, p):
        return pltpu.make_async_copy(hbuf.at[s, pl.ds(r, 1)], xs_ref.at[pl.ds(p, 1)], sem.at[s])

    def wait_slot(s):
        for _ in range(TOP_K):
            pltpu.make_async_copy(hbuf.at[s], xs_ref.at[pl.ds(0, tm)], sem.at[s]).wait()

    @pl.when(i >= 2)
    def _():
        wait_slot(slot)

    hbuf[slot] = _ln_mod(x_ref[...], sh_ref[0], sc_ref[0])

    def body(r, carry):
        base = (i * tm + r) * TOP_K
        for k in range(TOP_K):
            row_copy(slot, r, pos_ref[base + k]).start()
        return carry

    lax.fori_loop(0, tm, body, 0, unroll=8)

    @pl.when(i == n - 1)
    def _():
        wait_slot(slot)

    @pl.when((i == n - 1) & (i >= 1))
    def _():
        wait_slot(1 - slot)


def moe_dispatch(pos, x2, shift, scale, n_rows, *, seq, tm=512):
    t_tok, d = x2.shape
    tm = min(tm, seq)
    nt = seq // tm
    mod_spec = pl.BlockSpec((1, 1, d), lambda i, p: (i // nt, 0, 0))
    return pl.pallas_call(
        functools.partial(_dispatch_kernel, tm=tm),
        grid_spec=pltpu.PrefetchScalarGridSpec(
            num_scalar_prefetch=1,
            grid=(t_tok // tm,),
            in_specs=[pl.BlockSpec((tm, d), lambda i, p: (i, 0)), mod_spec, mod_spec,
                      pl.BlockSpec(memory_space=pl.ANY)],
            out_specs=pl.BlockSpec(memory_space=pl.ANY),
            scratch_shapes=[pltpu.VMEM((2, tm, d), F32), pltpu.SemaphoreType.DMA((2,))]),
        out_shape=jax.ShapeDtypeStruct((n_rows, d), F32),
        input_output_aliases={4: 0},
        compiler_params=_cparams("arbitrary"),
        name="moe_dispatch",
    )(pos, x2, shift, scale, jnp.zeros((n_rows, d), F32))


def _expert_kernel(te_ref, na_ref, xs_ref, w1_ref, w3_ref, w2_ref, o_ref, xb_ref):
    t = pl.program_id(0)
    k = pl.program_id(1)
    active = t < na_ref[0]

    @pl.when(active)
    def _():
        @pl.when(k == 0)
        def _():
            xb_ref[...] = xs_ref[...].astype(BF16)

        xb = xb_ref[...]
        act = _silu(_bdot(xb, w1_ref[0])) * _bdot(xb, w3_ref[0])
        y = _bdot(act.astype(BF16), w2_ref[0])

        @pl.when(k == 0)
        def _():
            o_ref[...] = y

        @pl.when(k > 0)
        def _():
            o_ref[...] += y

    @pl.when(jnp.logical_not(active) & (k == 0))
    def _():
        o_ref[...] = jnp.zeros_like(o_ref)


def moe_experts(tile_expert, n_active, xs, w1, w3, w2, *, tg, tk=512):
    n_rows, d = xs.shape
    dff = w1.shape[-1]
    nk = dff // tk
    kk = lambda t, k, na: jnp.where(t < na[0], k, nk - 1)
    return pl.pallas_call(
        _expert_kernel,
        grid_spec=pltpu.PrefetchScalarGridSpec(
            num_scalar_prefetch=2,
            grid=(n_rows // tg, nk),
            in_specs=[pl.BlockSpec((tg, d), lambda t, k, te, na: (t, 0)),
                      pl.BlockSpec((1, d, tk), lambda t, k, te, na: (te[t], 0, kk(t, k, na))),
                      pl.BlockSpec((1, d, tk), lambda t, k, te, na: (te[t], 0, kk(t, k, na))),
                      pl.BlockSpec((1, tk, d), lambda t, k, te, na: (te[t], kk(t, k, na), 0))],
            out_specs=pl.BlockSpec((tg, d), lambda t, k, te, na: (t, 0)),
            scratch_shapes=[pltpu.VMEM((tg, d), BF16)]),
        out_shape=jax.ShapeDtypeStruct((n_rows, d), F32),
        compiler_params=_cparams("arbitrary", "arbitrary"),
        name="moe_experts",
    )(tile_expert, n_active, xs, w1, w3, w2)


def _combine_kernel(pos_ref, x_ref, gt_ref, rt_ref, lg_ref, lb_ref, ys_ref, o_ref, ybuf, sem, *, tm, alpha):
    i = pl.program_id(0)
    n = pl.num_programs(0)
    slot = i % 2

    def issue(step, s):
        def body(r, carry):
            base = (step * tm + r) * TOP_K
            for k in range(TOP_K):
                pltpu.make_async_copy(ys_ref.at[pl.ds(pos_ref[base + k], 1)], ybuf.at[s, k, pl.ds(r, 1)],
                                      sem.at[s]).start()
            return carry
        lax.fori_loop(0, tm, body, 0, unroll=8)

    @pl.when(i == 0)
    def _():
        issue(0, 0)

    @pl.when(i + 1 < n)
    def _():
        issue(i + 1, 1 - slot)

    for k in range(TOP_K):
        pltpu.make_async_copy(ys_ref.at[pl.ds(0, tm)], ybuf.at[slot, k], sem.at[slot]).wait()
    rt = rt_ref[...]
    y = rt[:, 2:3] * ybuf[slot, 0] + rt[:, 3:4] * ybuf[slot, 1]
    o_ref[...] = _ln(alpha * x_ref[...] + gt_ref[0] * y) * lg_ref[...] + lb_ref[...]


def moe_combine(pos, x2, gate, route, ln_g, ln_b, ys, *, seq, alpha, tm=512):
    t_tok, d = x2.shape
    tm = min(tm, seq)
    nt = seq // tm
    vec = pl.BlockSpec((1, d), lambda i, p: (0, 0))
    return pl.pallas_call(
        functools.partial(_combine_kernel, tm=tm, alpha=alpha),
        grid_spec=pltpu.PrefetchScalarGridSpec(
            num_scalar_prefetch=1,
            grid=(t_tok // tm,),
            in_specs=[pl.BlockSpec((tm, d), lambda i, p: (i, 0)),
                      pl.BlockSpec((1, 1, d), lambda i, p: (i // nt, 0, 0)),
                      pl.BlockSpec((tm, ROUTE_COLS), lambda i, p: (i, 0)), vec, vec,
                      pl.BlockSpec(memory_space=pl.ANY)],
            out_specs=pl.BlockSpec((tm, d), lambda i, p: (i, 0)),
            scratch_shapes=[pltpu.VMEM((2, TOP_K, tm, d), F32), pltpu.SemaphoreType.DMA((2,))]),
        out_shape=jax.ShapeDtypeStruct((t_tok, d), F32),
        compiler_params=_cparams("arbitrary"),
        name="moe_combine",
    )(pos, x2, gate, route, ln_g, ln_b, ys)


def _dispatch_plan(route, *, tg):
    ids = route[:, 0:TOP_K].astype(jnp.int32).reshape(-1)
    n_assign = ids.shape[0]
    onehot = (ids[:, None] == jnp.arange(N_EXPERTS, dtype=jnp.int32)[None, :]).astype(jnp.int32)
    csum = jnp.cumsum(onehot, axis=0)
    counts = csum[-1]
    padded = (counts + tg - 1) // tg * tg
    ends = jnp.cumsum(padded)
    starts = ends - padded
    pos = jnp.sum(onehot * (starts[None, :] + csum - 1), axis=1).astype(jnp.int32)
    n_tiles = n_assign // tg + N_EXPERTS
    tile_start = jnp.arange(n_tiles, dtype=jnp.int32) * tg
    tile_expert = jnp.minimum(jnp.sum(tile_start[:, None] >= ends[None, :], axis=1), N_EXPERTS - 1)
    n_active = (ends[-1] // tg).reshape(1)
    return pos, tile_expert.astype(jnp.int32), n_active.astype(jnp.int32), n_tiles * tg


def ffn_moe(x2, shift, scale, gate, router, w1, w3, w2, ln_g, ln_b, *, seq, alpha, tg=1024):
    route = moe_route(x2, shift, scale, router, seq=seq)
    pos, tile_expert, n_active, n_rows = _dispatch_plan(route, tg=tg)
    xs = moe_dispatch(pos, x2, shift, scale, n_rows, seq=seq)
    ys = moe_experts(tile_expert, n_active, xs, w1, w3, w2, tg=tg)
    return moe_combine(pos, x2, gate, route, ln_g, ln_b, ys, seq=seq, alpha=alpha)


def _dft_matrices(rows, cols, n):
    f = lax.broadcasted_iota(jnp.int32, (rows, cols), 0)
    t = lax.broadcasted_iota(jnp.int32, (rows, cols), 1)
    ang = ((f * t) % n).astype(F32) * (2.0 * math.pi / n)
    return jnp.cos(ang), jnp.sin(ang)


def _twiddles(seq, width):
    ang = jnp.arange(seq // 2, dtype=F32) * (math.pi / seq)
    return jnp.broadcast_to(jnp.stack([jnp.cos(ang), jnp.sin(ang)])[:, :, None], (2, seq // 2, width))


def _position_features(seq, n_bands, width):
    t = jnp.arange(seq, dtype=F32)[:, None]
    tn = t / max(seq - 1, 1)
    bands = jnp.arange(1, n_bands + 1, dtype=F32)
    ang = t * bands * (2.0 * math.pi / seq)
    z = jnp.concatenate([tn, jnp.cos(ang), jnp.sin(ang)], axis=-1)
    return jnp.pad(z, ((0, 0), (0, width - z.shape[1])))


def _s5_discretize(a_re, a_im, log_dt, b_re, b_im):
    lam_re = jnp.minimum(a_re, S5_MAX_RE)
    lam_im = a_im
    dt = jnp.exp(log_dt)[..., None]
    mag = jnp.exp(lam_re * dt)
    ang = lam_im * dt
    abar_re = mag * jnp.cos(ang)
    abar_im = mag * jnp.sin(ang)
    den = lam_re * lam_re + lam_im * lam_im
    q_re = ((abar_re - 1.0) * lam_re + abar_im * lam_im) / den
    q_im = (abar_im * lam_re - (abar_re - 1.0) * lam_im) / den
    bb_re = q_re[..., None] * b_re - q_im[..., None] * b_im
    bb_im = q_re[..., None] * b_im + q_im[..., None] * b_re
    return abar_re, abar_im, bb_re, bb_im


def _s5_matrices(a_re, a_im, log_dt, b_re, b_im, c_re, c_im):
    abr, abi, bbr, bbi = _s5_discretize(a_re, a_im, log_dt, b_re, b_im)
    g_all, p, k = bbr.shape[1:]
    gb = S5_BLOCK_CH // S5_GROUP
    nj = g_all // gb
    eye = jnp.eye(gb, dtype=F32)

    def in_mat(bb):
        t = jnp.swapaxes(bb, -1, -2).reshape(2, nj, gb, k, p)
        return jnp.einsum("djgkp,gh->djgkhp", t, eye).reshape(2, nj, gb * k, gb * p)

    def out_mat(cc):
        t = jnp.swapaxes(cc, -1, -2).reshape(2, nj, gb, p, k)
        return jnp.einsum("djhpc,hg->djhpgc", t, eye).reshape(2, nj, gb * p, gb * k)

    bmat = jnp.concatenate([in_mat(bbr), in_mat(bbi)], axis=-1).astype(BF16)
    cmat = jnp.concatenate([out_mat(c_re), -out_mat(c_im)], axis=-2).astype(BF16)
    abar = jnp.stack([abr.reshape(2, nj, gb * p), abi.reshape(2, nj, gb * p)], axis=2)
    return bmat, cmat, abar


def kernel(x, c, ctx, c_ctx, w_mod, b_mod, w_in, hy_short_w, hy_short_b, hy_f_w1, hy_f_b1, hy_f_w2, hy_f_b2, hy_f_w3, hy_f_b3, hy_freq, hy_decay, hy_bias, w_hy_out, cf_dw_w, cf_dw_b, cf_ln_g, cf_ln_b, w_cf_out, s5_a_re, s5_a_im, s5_log_dt, s5_b_re, s5_b_im, s5_c_re, s5_c_im, s5_d, s5_w_glu, s5_b_glu, w_s5_out, w_o, ln1_g, ln1_b, ln2_g, ln2_b, ffn_w1, ffn_w3, ffn_w2, moe_router, moe_w1, moe_w3, moe_w2):
    bsz, seq, d = x.shape
    seq_c = ctx.shape[1]
    depth = w_mod.shape[0]
    alpha = (2.0 * depth) ** 0.25
    ch = D_BRANCH
    hy_cols = (HY_ORDER + 1) * ch
    cf_off = hy_cols
    s5_off = cf_off + 2 * ch
    gate_off = s5_off + ch
    n_bands = (hy_f_w1.shape[1] - 1) // 2
    nj = ch // S5_BLOCK_CH

    x2 = x.reshape(bsz * seq, d)
    xc2 = ctx.reshape(bsz * seq_c, d)
    mod_rows = 24
    cc = jnp.zeros((mod_rows, d), F32).at[:bsz].set(c).at[bsz].set(c_ctx)

    dft = {}
    feats = {}
    for s in (seq, seq_c):
        cs, ss = _dft_matrices(s // 2, s, 2 * s)
        csh, ssh = _dft_matrices(s // 2, s // 2, s)
        dft[s] = (cs, ss, csh.astype(BF16), ssh.astype(BF16), _twiddles(s, ch))
        feats[s] = _position_features(s, n_bands, 128)
    zero_state = jnp.zeros((2, nj, bsz, 2 * S5_BLOCK_ST), F32)
    row = lambda v: v[None, :]

    for l in range(depth):
        last = l == depth - 1
        mods = mod_vectors(cc, w_mod[l].astype(BF16), row(b_mod[l])).reshape(mod_rows, 6, d)
        lat = [mods[:bsz, i][:, None, :] for i in range(6)]
        cxm = [jnp.broadcast_to(mods[bsz, i][None, None, :], (bsz, 1, d)) for i in range(6)]

        w_in_b = w_in[l].astype(BF16)
        w_mix = w_in_b[:, :gate_off]
        w_s5 = w_in_b[:, s5_off:gate_off]
        w_gate = w_in_b[:, gate_off:]
        sw = jnp.concatenate([hy_short_w[l][:, i * ch:(i + 1) * ch] for i in range(HY_ORDER + 1)], axis=0)
        sb = hy_short_b[l].reshape(HY_ORDER + 1, ch)
        w1p = jnp.pad(hy_f_w1[l], ((0, 128 - hy_f_w1.shape[1]), (0, 0)))
        bmat, cmat, abar = _s5_matrices(s5_a_re[l], s5_a_im[l], s5_log_dt[l], s5_b_re[l], s5_b_im[l],
                                        s5_c_re[l], s5_c_im[l])
        merge_w = (w_gate, w_hy_out[l].astype(BF16), w_cf_out[l].astype(BF16), row(s5_d[l]),
                   s5_w_glu[l].astype(BF16), row(s5_b_glu[l]), w_s5_out[l].astype(BF16),
                   w_o[l].astype(BF16), row(ln1_g[l]), row(ln1_b[l]))

        def spectra(s):
            filt = hyena_filters(feats[s], w1p, row(hy_f_b1[l]), hy_f_w2[l], row(hy_f_b2[l]), hy_f_w3[l],
                                 row(hy_f_b3[l]), hy_freq[l], row(hy_decay[l]))
            return hyena_spectra(dft[s][0], dft[s][1], filt)

        def token_mixer(xs, s, seg, m, init, shared):
            sh1, sc1, g1 = m[0], m[1], m[2]
            p = ln_mod_proj(xs, sh1, sc1, w_mix, seq=s, shared_mod=shared)
            z_hy = hyena_mix(p, sw, sb, dft[s][2], dft[s][3], dft[s][4], *spectra(s), hy_bias[l], seq=s)
            z_cf = conformer_mix(p, cf_dw_w[l], row(cf_dw_b[l]), row(cf_ln_g[l]), row(cf_ln_b[l]), seg=seg,
                                 col_block=cf_off // ch)
            ys, fin = s5_scan(p, bmat, cmat, abar, init, seq=s, col_off=s5_off)
            x1 = merge_branches(xs, sh1, sc1, g1, z_hy, z_cf, ys, p, *merge_w, seq=s, alpha=alpha,
                                s5_col_block=s5_off // ch, shared_mod=shared)
            return x1, fin

        def channel_mixer(xs, s, m, shared):
            sh2, sc2, g2 = m[3], m[4], m[5]
            i = l // 2
            if l % 2 == 0:
                return ffn_dense(xs, sh2, sc2, g2, ffn_w1[i].astype(BF16), ffn_w3[i].astype(BF16),
                                 ffn_w2[i].astype(BF16), row(ln2_g[l]), row(ln2_b[l]), seq=s, alpha=alpha,
                                 shared_mod=shared)
            router = jnp.pad(moe_router[i], ((0, 0), (0, ROUTER_LANES - moe_router.shape[-1])))
            return ffn_moe(xs, sh2, sc2, g2, router, moe_w1[i].astype(BF16), moe_w3[i].astype(BF16),
                           moe_w2[i].astype(BF16), row(ln2_g[l]), row(ln2_b[l]), seq=s, alpha=alpha)

        if last:
            u_s5 = ln_mod_proj(xc2, cxm[0], cxm[1], w_s5, seq=seq_c, shared_mod=True)
            _, finals = s5_scan(u_s5, bmat, cmat, abar, zero_state, seq=seq_c, col_off=0)
        else:
            xc2, finals = token_mixer(xc2, seq_c, seq_c, cxm, zero_state, True)
            xc2 = channel_mixer(xc2, seq_c, cxm, True)

        x2, _ = token_mixer(x2, seq, GRID_W, lat, finals, False)
        x2 = channel_mixer(x2, seq, lat, False)
    return x2.reshape(bsz, seq, d)
```

```python
import functools
import math

import jax
import jax.numpy as jnp
from jax import lax
from jax.experimental import pallas as pl
from jax.experimental.pallas import tpu as pltpu

F32 = jnp.float32
BF16 = jnp.bfloat16
HIGHEST = lax.Precision.HIGHEST

V7X_VMEM_LIMIT_BYTES = 56 * 1024 * 1024

LN_EPS = 1e-5
D_BRANCH = 512
HY_ORDER = 2
HY_SHORT = 3
HY_ROW_BLOCK = 512
CF_KERNEL = 31
SUBLANES = 8
LANES = 128
CF_PAD = 16
CF_SHIFT_SPAN = 24
GRID_W = 64
S5_GROUP = 16
S5_STATE = 64
S5_MAX_RE = -1e-4
S5_TIME_BLOCK = 16
S5_SUB_CHUNK = 32
S5_BLOCK_CH = 128
S5_BLOCK_ST = S5_BLOCK_CH // S5_GROUP * S5_STATE
N_EXPERTS = 8
TOP_K = 2
ROUTER_LANES = 128
ROUTE_COLS = 8


def _cparams(*sem):
    return pltpu.CompilerParams(dimension_semantics=sem, vmem_limit_bytes=V7X_VMEM_LIMIT_BYTES)


def _const_spec(shape):
    nd = len(shape)
    return pl.BlockSpec(shape, lambda *_: (0,) * nd, pipeline_mode=pl.Buffered(1))


def _bdot(a, b):
    return jnp.dot(a, b, preferred_element_type=F32)


def _sigmoid(x):
    return 1.0 / (1.0 + jnp.exp(-x))


def _silu(x):
    return x * _sigmoid(x)


def _gelu_tanh(x):
    return 0.5 * x * (1.0 + jnp.tanh(math.sqrt(2.0 / math.pi) * (x + 0.044715 * (x * x * x))))


def _ln(x):
    mu = jnp.mean(x, axis=-1, keepdims=True)
    xc = x - mu
    var = jnp.mean(xc * xc, axis=-1, keepdims=True)
    return xc * lax.rsqrt(var + LN_EPS)


def _ln_mod(x, shift, scale):
    return _ln(x) * (1.0 + scale) + shift


def _token_tiling(t_tok, seq, cap, shared_mod):
    if shared_mod:
        return min(t_tok, cap), lambda i: 0
    tm = min(seq, cap)
    nt = seq // tm
    return tm, lambda i: i // nt


def _mod_kernel(c_ref, w_ref, b_ref, o_ref):
    s = _silu(c_ref[...])
    o_ref[...] = _bdot(s.astype(BF16), w_ref[...]) + b_ref[...]


def mod_vectors(cc, w, b, *, tn=1536):
    r, d = cc.shape
    n = w.shape[1]
    return pl.pallas_call(
        _mod_kernel,
        grid=(n // tn,),
        in_specs=[pl.BlockSpec((r, d), lambda j: (0, 0)),
                  pl.BlockSpec((d, tn), lambda j: (0, j)),
                  pl.BlockSpec((1, tn), lambda j: (0, j))],
        out_specs=pl.BlockSpec((r, tn), lambda j: (0, j)),
        out_shape=jax.ShapeDtypeStruct((r, n), F32),
        compiler_params=_cparams("arbitrary"),
        name="mod_vectors",
    )(cc, w, b)


def _proj_kernel(x_ref, sh_ref, sc_ref, w_ref, o_ref, *, tn):
    h = _ln_mod(x_ref[...], sh_ref[0], sc_ref[0]).astype(BF16)
    for j in range(o_ref.shape[1] // tn):
        cols = slice(j * tn, (j + 1) * tn)
        o_ref[:, cols] = _bdot(h, w_ref[:, cols])


def ln_mod_proj(x2, shift, scale, w, *, seq, shared_mod=False, tn=512):
    t_tok, d = x2.shape
    n = w.shape[1]
    tm, brow = _token_tiling(t_tok, seq, 1024, shared_mod)
    mod_spec = pl.BlockSpec((1, 1, d), lambda i: (brow(i), 0, 0))
    return pl.pallas_call(
        functools.partial(_proj_kernel, tn=tn),
        grid=(t_tok // tm,),
        in_specs=[pl.BlockSpec((tm, d), lambda i: (i, 0)), mod_spec, mod_spec, _const_spec((d, n))],
        out_specs=pl.BlockSpec((tm, n), lambda i: (i, 0)),
        out_shape=jax.ShapeDtypeStruct((t_tok, n), F32),
        compiler_params=_cparams("arbitrary"),
        name="ln_mod_proj",
    )(x2, shift, scale, w)


def _filter_kernel(z_ref, w1_ref, b1_ref, w2_ref, b2_ref, w3_ref, b3_ref, fr_ref, dec_ref, o_ref):
    z = z_ref[...]
    h = jnp.sin(fr_ref[0:1, :] * (jnp.dot(z, w1_ref[...], precision=HIGHEST, preferred_element_type=F32)
                                  + b1_ref[...]))
    h = jnp.sin(fr_ref[1:2, :] * (jnp.dot(h, w2_ref[...], precision=HIGHEST, preferred_element_type=F32)
                                  + b2_ref[...]))
    h = jnp.dot(h, w3_ref[...], precision=HIGHEST, preferred_element_type=F32) + b3_ref[...]
    tn = z[:, 0:1]
    o_ref[...] = h * jnp.exp(-tn * jnp.abs(dec_ref[...]))


def hyena_filters(z, w1, b1, w2, b2, w3, b3, freq, decay, *, tn=512):
    seq, zf = z.shape
    hid = w2.shape[0]
    n = w3.shape[1]
    full = lambda shape: pl.BlockSpec(shape, lambda j: (0, 0))
    return pl.pallas_call(
        _filter_kernel,
        grid=(n // tn,),
        in_specs=[full((seq, zf)), full((zf, hid)), full((1, hid)), full((hid, hid)), full((1, hid)),
                  pl.BlockSpec((hid, tn), lambda j: (0, j)), pl.BlockSpec((1, tn), lambda j: (0, j)),
                  full((2, hid)), pl.BlockSpec((1, tn), lambda j: (0, j))],
        out_specs=pl.BlockSpec((seq, tn), lambda j: (0, j)),
        out_shape=jax.ShapeDtypeStruct((seq, n), F32),
        compiler_params=_cparams("arbitrary"),
        name="hyena_filters",
    )(z, w1, b1, w2, b2, w3, b3, freq, decay)


def _spectrum_kernel(cs_ref, ss_ref, filt_ref, k1r_ref, k1i_ref, k2r_ref, k2i_ref, km_ref, *, tf, n_fft):
    ch = k1r_ref.shape[-1]
    seq = filt_ref.shape[0]
    hf = filt_ref[:, 0:ch]
    tau = lax.broadcasted_iota(jnp.int32, (seq, 1), 0)
    hb = jnp.where(tau == 0, 0.0, filt_ref[:, ch:2 * ch])
    s_cos = hf + hb
    s_sin = hb - hf
    sign = (1 - 2 * (tau & 1)).astype(F32)
    f = pl.program_id(1) * tf + lax.broadcasted_iota(jnp.int32, (tf, 1), 0)
    wgt = jnp.where(f == 0, 1.0, 2.0) / n_fft
    hdot = lambda a, b: jnp.dot(a, b, precision=HIGHEST, preferred_element_type=F32)
    cs = cs_ref[...]
    ss = ss_ref[...]
    k1r_ref[0] = wgt * hdot(cs, s_cos)
    k1i_ref[0] = wgt * hdot(ss, s_sin)
    k2r_ref[0] = wgt * hdot(cs, sign * s_cos)
    k2i_ref[0] = -wgt * hdot(ss, sign * s_sin)
    quarter = tau & 3
    c4 = jnp.where(quarter == 0, 1.0, jnp.where(quarter == 2, -1.0, 0.0))
    s4 = jnp.where(quarter == 1, 1.0, jnp.where(quarter == 3, -1.0, 0.0))
    km_ref[0, 0:1, :] = jnp.sum(c4 * s_cos, axis=0, keepdims=True) * (2.0 / n_fft)
    km_ref[0, 1:2, :] = jnp.sum(s4 * s_sin, axis=0, keepdims=True) * (2.0 / n_fft)


def hyena_spectra(cs, ss, filt, *, tf=256):
    m, seq = cs.shape
    ch = D_BRANCH
    tf = min(tf, m)
    rows = pl.BlockSpec((tf, seq), lambda o, f: (f, 0))
    out = pl.BlockSpec((1, tf, ch), lambda o, f: (o, f, 0))
    return pl.pallas_call(
        functools.partial(_spectrum_kernel, tf=tf, n_fft=2 * seq),
        grid=(HY_ORDER, m // tf),
        in_specs=[rows, rows, pl.BlockSpec((seq, 2 * ch), lambda o, f: (0, o))],
        out_specs=[out] * 4 + [pl.BlockSpec((1, 2, ch), lambda o, f: (o, 0, 0))],
        out_shape=[jax.ShapeDtypeStruct((HY_ORDER, m, ch), F32)] * 4
                  + [jax.ShapeDtypeStruct((HY_ORDER, 2, ch), F32)],
        compiler_params=_cparams("arbitrary", "arbitrary"),
        name="hyena_spectra",
    )(cs, ss, filt)


def _hyena_kernel(*refs, nh):
    parts = [refs[p * nh:(p + 1) * nh] for p in range(HY_ORDER + 1)]
    (sw_ref, sb_ref, cs_ref, ss_ref, tw_ref, k1r_ref, k1i_ref, k2r_ref, k2i_ref, km_ref, bias_ref,
     o_ref, ze_ref, zo_ref, xe_ref, xo_ref, zeb_ref, zob_ref, g_ref, il_ref) = refs[(HY_ORDER + 1) * nh:]
    m = o_ref.shape[0] // 2
    row = lax.broadcasted_iota(jnp.int32, (m, 1), 0)
    sign = (1 - 2 * (row & 1)).astype(F32)

    def short_conv(part):
        pe = jnp.concatenate([r[pl.ds(0, m, stride=2), :] for r in parts[part]], axis=1)
        po = jnp.concatenate([r[pl.ds(1, m, stride=2), :] for r in parts[part]], axis=1)
        po_prev = jnp.where(row == 0, 0.0, pltpu.roll(po, 1, 0))
        pe_next = jnp.where(row == m - 1, 0.0, pltpu.roll(pe, m - 1, 0))
        w = sw_ref[part * HY_SHORT:(part + 1) * HY_SHORT, :]
        b = sb_ref[part:part + 1, :]
        return (w[0:1] * po_prev + w[1:2] * pe + w[2:3] * po + b,
                w[0:1] * pe + w[1:2] * po + w[2:3] * pe_next + b)

    rb = min(m, HY_ROW_BLOCK)
    blocks = [slice(r * rb, (r + 1) * rb) for r in range(m // rb)]
    ze_ref[...], zo_ref[...] = short_conv(HY_ORDER)
    for o in range(HY_ORDER):
        ze = ze_ref[...]
        zo = zo_ref[...]
        zeb_ref[...] = ze.astype(BF16)
        zob_ref[...] = zo.astype(BF16)
        e_m = jnp.sum(sign * ze, axis=0, keepdims=True)
        o_m = jnp.sum(sign * zo, axis=0, keepdims=True)
        kmr = km_ref[o, 0:1, :]
        kmi = km_ref[o, 1:2, :]
        nyq_e = e_m * kmr + o_m * kmi
        nyq_o = o_m * kmr - e_m * kmi
        bias = bias_ref[o:o + 1, :]
        xe_ref[...], xo_ref[...] = short_conv(o)
        for rows in blocks:
            cs = cs_ref[rows, :]
            ss = ss_ref[rows, :]
            ae = _bdot(cs, zeb_ref[...])
            be = _bdot(ss, zeb_ref[...])
            ao = _bdot(cs, zob_ref[...])
            bo = _bdot(ss, zob_ref[...])
            c = tw_ref[0, rows, :]
            s = tw_ref[1, rows, :]
            tr = c * ao - s * bo
            ti = -(c * bo + s * ao)
            pr, pi = ae + tr, ti - be
            qr, qi = ae - tr, -be - ti
            k1r, k1i = k1r_ref[o, rows, :], k1i_ref[o, rows, :]
            k2r, k2i = k2r_ref[o, rows, :], k2i_ref[o, rows, :]
            z1r = pr * k1r - pi * k1i
            z1i = pr * k1i + pi * k1r
            z2r = qr * k2r + qi * k2i
            z2i = qr * k2i - qi * k2r
            dr, di = z1r - z2r, z1i + z2i
            g_ref[0, rows, :] = (z1r + z2r).astype(BF16)
            g_ref[1, rows, :] = (z1i - z2i).astype(BF16)
            g_ref[2, rows, :] = (c * dr - s * di).astype(BF16)
            g_ref[3, rows, :] = (c * di + s * dr).astype(BF16)
        for rows in blocks:
            cs = cs_ref[rows, :]
            ss = ss_ref[rows, :]
            ye = _bdot(cs, g_ref[0]) - _bdot(ss, g_ref[1]) + sign[rows] * nyq_e + bias * ze_ref[rows, :]
            yo = _bdot(cs, g_ref[2]) - _bdot(ss, g_ref[3]) + sign[rows] * nyq_o + bias * zo_ref[rows, :]
            ze_ref[rows, :] = xe_ref[rows, :] * ye
            zo_ref[rows, :] = xo_ref[rows, :] * yo
    for h in range(nh):
        lanes = slice(h * LANES, (h + 1) * LANES)
        il_ref[pl.ds(0, m, stride=2), :] = ze_ref[:, lanes]
        il_ref[pl.ds(1, m, stride=2), :] = zo_ref[:, lanes]
        o_ref[:, lanes] = il_ref[...]


def hyena_mix(p, sw, sb, cs, ss, tw, k1r, k1i, k2r, k2i, km, bias, *, seq, cb=256):
    t_tok = p.shape[0]
    ch = D_BRANCH
    bsz = t_tok // seq
    m = seq // 2
    nh = cb // LANES
    col_blocks = ch // LANES
    cols = [pl.BlockSpec((seq, LANES), functools.partial(lambda j, b, c0: (b, c0 + j * nh), c0=part * col_blocks + h))
            for part in range(HY_ORDER + 1) for h in range(nh)]
    per_c = lambda rows: pl.BlockSpec((rows, cb), lambda j, b: (0, j))
    spec3 = lambda lead, rows: pl.BlockSpec((lead, rows, cb), lambda j, b: (0, 0, j),
                                            pipeline_mode=pl.Buffered(1))
    return pl.pallas_call(
        functools.partial(_hyena_kernel, nh=nh),
        grid=(ch // cb, bsz),
        in_specs=cols + [per_c(3 * HY_SHORT), per_c(3), _const_spec((m, m)), _const_spec((m, m)),
                         spec3(2, m)] + [spec3(HY_ORDER, m)] * 4 + [spec3(HY_ORDER, 2), per_c(HY_ORDER)],
        out_specs=pl.BlockSpec((seq, cb), lambda j, b: (b, j)),
        out_shape=jax.ShapeDtypeStruct((t_tok, ch), F32),
        scratch_shapes=[pltpu.VMEM((m, cb), F32)] * 4 + [pltpu.VMEM((m, cb), BF16)] * 2
                       + [pltpu.VMEM((4, m, cb), BF16), pltpu.VMEM((seq, LANES), F32)],
        compiler_params=_cparams("arbitrary", "arbitrary"),
        name="hyena_mix",
    )(*([p] * ((HY_ORDER + 1) * nh)), sw, sb, cs, ss, tw, k1r, k1i, k2r, k2i, km, bias)


def _conformer_kernel(a_ref, gl_ref, w_ref, b_ref, g_ref, be_ref, o_ref, pad_ref, shift_ref, *, seg, nseg):
    ch = a_ref.shape[-1]
    zeros = jnp.zeros((nseg, CF_PAD, ch), F32)
    pad_ref[:, 0:CF_PAD, :] = zeros
    pad_ref[:, CF_PAD + seg:, :] = zeros
    u = a_ref[...] * _sigmoid(gl_ref[...])
    pad_ref[:, CF_PAD:CF_PAD + seg, :] = u.reshape(nseg, seg, ch)
    span = seg + CF_SHIFT_SPAN
    for r in range(1, SUBLANES):
        shift_ref[r - 1] = pad_ref[:, r:r + span, :]
    half = (CF_KERNEL - 1) // 2

    def segment(s, carry):
        for c in range(ch // LANES):
            lanes = slice(c * LANES, (c + 1) * LANES)
            acc = jnp.zeros((seg, LANES), F32)
            for k in range(CF_KERNEL):
                q, r = divmod(CF_PAD - half + k, SUBLANES)
                rows = slice(q * SUBLANES, q * SUBLANES + seg)
                tap = pad_ref[s, rows, lanes] if r == 0 else shift_ref[r - 1, s, rows, lanes]
                acc = acc + w_ref[k:k + 1, lanes] * tap
            o_ref[pl.ds(pl.multiple_of(s * seg, seg), seg), lanes] = acc
        return carry

    lax.fori_loop(0, nseg, segment, 0)
    y = _ln(o_ref[...] + b_ref[...]) * g_ref[...] + be_ref[...]
    o_ref[...] = _silu(y)


def conformer_mix(p, dw_w, dw_b, ln_g, ln_b, *, seg, col_block):
    t_tok = p.shape[0]
    ch = D_BRANCH
    tm = max(seg, 512)
    nseg = tm // seg
    vec = lambda rows: pl.BlockSpec((rows, ch), lambda i: (0, 0))
    return pl.pallas_call(
        functools.partial(_conformer_kernel, seg=seg, nseg=nseg),
        grid=(t_tok // tm,),
        in_specs=[pl.BlockSpec((tm, ch), lambda i: (i, col_block)),
                  pl.BlockSpec((tm, ch), lambda i: (i, col_block + 1)),
                  vec(CF_KERNEL), vec(1), vec(1), vec(1)],
        out_specs=pl.BlockSpec((tm, ch), lambda i: (i, 0)),
        out_shape=jax.ShapeDtypeStruct((t_tok, ch), F32),
        scratch_shapes=[pltpu.VMEM((nseg, seg + 2 * CF_PAD, ch), F32),
                        pltpu.VMEM((SUBLANES - 1, nseg, seg + CF_SHIFT_SPAN, ch), F32)],
        compiler_params=_cparams("arbitrary"),
        name="conformer_mix",
    )(p, p, dw_w, dw_b, ln_g, ln_b)


def _s5_kernel(u_ref, b_ref, c_ref, a_ref, init_ref, y_ref, fin_ref, u1_ref, utm_ref, bu_ref, sb_ref,
               ytm_ref, st_ref, *, tc, bsz, reverse):
    c = pl.program_id(1)
    ns = S5_BLOCK_ST

    @pl.when(c == 0)
    def _():
        st_ref[...] = init_ref[0]

    tb = S5_TIME_BLOCK
    for th in range(tc // tb):
        for b in range(bsz):
            u1_ref[pl.ds((th * bsz + b) * tb, tb), :] = u_ref[b, th * tb:(th + 1) * tb, :]
    for th in range(tc // tb):
        for tl in range(tb):
            utm_ref[pl.ds((th * tb + tl) * bsz, bsz), :] = u1_ref[pl.ds(th * bsz * tb + tl, bsz, stride=tb), :]
    a_re = jnp.broadcast_to(a_ref[0, 0:1, :], (bsz, ns))
    a_im = jnp.broadcast_to(a_ref[0, 1:2, :], (bsz, ns))
    s_re = st_ref[:, 0:ns]
    s_im = st_ref[:, ns:2 * ns]
    sub = min(tc, S5_SUB_CHUNK)
    order = (lambda n: range(n - 1, -1, -1)) if reverse else range
    for sc in order(tc // sub):
        r0 = sc * sub * bsz
        rows = slice(r0, r0 + sub * bsz)
        bu_ref[rows, :] = _bdot(utm_ref[rows, :].astype(BF16), b_ref[0])
        for tl in order(sub):
            rr = slice(r0 + tl * bsz, r0 + (tl + 1) * bsz)
            n_re = a_re * s_re - a_im * s_im + bu_ref[rr, 0:ns]
            n_im = a_re * s_im + a_im * s_re + bu_ref[rr, ns:2 * ns]
            sb_ref[rr, 0:ns] = n_re.astype(BF16)
            sb_ref[rr, ns:2 * ns] = n_im.astype(BF16)
            s_re, s_im = n_re, n_im
        ytm_ref[rows, :] = _bdot(sb_ref[rows, :], c_ref[0])
    st_ref[:, 0:ns] = s_re
    st_ref[:, ns:2 * ns] = s_im
    for b in range(bsz):
        y_ref[b] = ytm_ref[pl.ds(b, tc, stride=bsz), :]

    @pl.when(c == pl.num_programs(1) - 1)
    def _():
        fin_ref[0] = st_ref[...]


def s5_scan(p, bmat, cmat, abar, init, *, seq, col_off, tc=128):
    t_tok, n_cols = p.shape
    ch = D_BRANCH
    bsz = t_tok // seq
    u3 = p.reshape(bsz, seq, n_cols)
    j0 = col_off // S5_BLOCK_CH
    nj = ch // S5_BLOCK_CH
    tc = min(tc, seq)
    nc = seq // tc
    rows = tc * bsz
    ns2 = 2 * S5_BLOCK_ST
    par = lambda r, cdim: pl.BlockSpec((1, r, cdim), lambda j, c: (j, 0, 0))
    ys, fins = [], []
    for d, reverse in enumerate((False, True)):
        chunk = (lambda c: nc - 1 - c) if reverse else (lambda c: c)
        y, fin = pl.pallas_call(
            functools.partial(_s5_kernel, tc=tc, bsz=bsz, reverse=reverse),
            grid=(nj, nc),
            in_specs=[pl.BlockSpec((bsz, tc, S5_BLOCK_CH),
                                   functools.partial(lambda j, c, ck: (0, ck(c), j0 + j), ck=chunk)),
                      par(S5_BLOCK_CH, ns2), par(ns2, S5_BLOCK_CH), par(2, S5_BLOCK_ST), par(bsz, ns2)],
            out_specs=[pl.BlockSpec((bsz, tc, S5_BLOCK_CH),
                                    functools.partial(lambda j, c, ck: (0, ck(c), j), ck=chunk)),
                       par(bsz, ns2)],
            out_shape=[jax.ShapeDtypeStruct((bsz, seq, ch), F32),
                       jax.ShapeDtypeStruct((nj, bsz, ns2), F32)],
            scratch_shapes=[pltpu.VMEM((rows, S5_BLOCK_CH), F32), pltpu.VMEM((rows, S5_BLOCK_CH), F32),
                            pltpu.VMEM((rows, ns2), F32), pltpu.VMEM((rows, ns2), BF16),
                            pltpu.VMEM((rows, S5_BLOCK_CH), F32), pltpu.VMEM((bsz, ns2), F32)],
            compiler_params=_cparams("arbitrary", "arbitrary"),
            name="s5_scan_bwd" if reverse else "s5_scan_fwd",
        )(u3, bmat[d], cmat[d], abar[d], init[d])
        ys.append(y.reshape(t_tok, ch))
        fins.append(fin)
    return ys, jnp.stack(fins)


def _merge_kernel(x_ref, sh_ref, sc_ref, gt_ref, zh_ref, zc_ref, ysf_ref, ysb_ref, us_ref,
                  wg_ref, why_ref, wcf_ref, ds_ref, wglu_ref, bglu_ref, ws5_ref, wo_ref,
                  lg_ref, lb_ref, o_ref, *, alpha):
    d = x_ref.shape[-1]
    x = x_ref[...]
    h = _ln_mod(x, sh_ref[0], sc_ref[0]).astype(BF16)
    gate = lambda i: _sigmoid(_bdot(h, wg_ref[:, i * d:(i + 1) * d]))
    mix = gate(0) * _bdot(zh_ref[...].astype(BF16), why_ref[...])
    mix = mix + gate(1) * _bdot(zc_ref[...].astype(BF16), wcf_ref[...])
    ys = ysf_ref[...] + ysb_ref[...] + ds_ref[...] * us_ref[...]
    g = _gelu_tanh(ys)
    y5 = g * _sigmoid(_bdot(g.astype(BF16), wglu_ref[...]) + bglu_ref[...])
    mix = mix + gate(2) * _bdot(y5.astype(BF16), ws5_ref[...])
    y = _bdot(mix.astype(BF16), wo_ref[...])
    o_ref[...] = _ln(alpha * x + gt_ref[0] * y) * lg_ref[...] + lb_ref[...]


def merge_branches(x2, shift, scale, gate, z_hy, z_cf, ys, p, w_gate, w_hy, w_cf, d_skip,
                   w_glu, b_glu, w_s5, w_o, ln_g, ln_b, *, seq, alpha, s5_col_block, shared_mod=False):
    t_tok, d = x2.shape
    ch = D_BRANCH
    tm, brow = _token_tiling(t_tok, seq, 512, shared_mod)
    mod_spec = pl.BlockSpec((1, 1, d), lambda i: (brow(i), 0, 0))
    tok = lambda n: pl.BlockSpec((tm, n), lambda i: (i, 0))
    return pl.pallas_call(
        functools.partial(_merge_kernel, alpha=alpha),
        grid=(t_tok // tm,),
        in_specs=[tok(d), mod_spec, mod_spec, mod_spec, tok(ch), tok(ch), tok(ch), tok(ch),
                  pl.BlockSpec((tm, ch), lambda i: (i, s5_col_block)),
                  _const_spec((d, 3 * d)), _const_spec((ch, d)), _const_spec((ch, d)),
                  _const_spec((1, ch)), _const_spec((ch, ch)), _const_spec((1, ch)),
                  _const_spec((ch, d)), _const_spec((d, d)), _const_spec((1, d)), _const_spec((1, d))],
        out_specs=tok(d),
        out_shape=jax.ShapeDtypeStruct((t_tok, d), F32),
        compiler_params=_cparams("arbitrary"),
        name="merge_branches",
    )(x2, shift, scale, gate, z_hy, z_cf, ys[0], ys[1], p, w_gate, w_hy, w_cf, d_skip, w_glu, b_glu,
      w_s5, w_o, ln_g, ln_b)


def _ffn_kernel(x_ref, sh_ref, sc_ref, gt_ref, w1_ref, w3_ref, w2_ref, lg_ref, lb_ref, o_ref, acc_ref,
                *, alpha, tk):
    x = x_ref[...]
    h = _ln_mod(x, sh_ref[0], sc_ref[0]).astype(BF16)
    for k in range(w1_ref.shape[1] // tk):
        cols = slice(k * tk, (k + 1) * tk)
        act = _silu(_bdot(h, w1_ref[:, cols])) * _bdot(h, w3_ref[:, cols])
        y = _bdot(act.astype(BF16), w2_ref[cols, :])
        if k == 0:
            acc_ref[...] = y
        else:
            acc_ref[...] += y
    o_ref[...] = _ln(alpha * x + gt_ref[0] * acc_ref[...]) * lg_ref[...] + lb_ref[...]


def ffn_dense(x2, shift, scale, gate, w1, w3, w2, ln_g, ln_b, *, seq, alpha, shared_mod=False, tk=256):
    t_tok, d = x2.shape
    dff = w1.shape[1]
    tm, brow = _token_tiling(t_tok, seq, 1024, shared_mod)
    mod_spec = pl.BlockSpec((1, 1, d), lambda i: (brow(i), 0, 0))
    return pl.pallas_call(
        functools.partial(_ffn_kernel, alpha=alpha, tk=tk),
        grid=(t_tok // tm,),
        in_specs=[pl.BlockSpec((tm, d), lambda i: (i, 0)), mod_spec, mod_spec, mod_spec,
                  _const_spec((d, dff)), _const_spec((d, dff)), _const_spec((dff, d)),
                  _const_spec((1, d)), _const_spec((1, d))],
        out_specs=pl.BlockSpec((tm, d), lambda i: (i, 0)),
        out_shape=jax.ShapeDtypeStruct((t_tok, d), F32),
        scratch_shapes=[pltpu.VMEM((tm, d), F32)],
        compiler_params=_cparams("arbitrary"),
        name="ffn_dense",
    )(x2, shift, scale, gate, w1, w3, w2, ln_g, ln_b)


def _router_top2(hf, router):
    logits = jnp.dot(hf, router, precision=HIGHEST, preferred_element_type=F32)
    lane = lax.broadcasted_iota(jnp.int32, logits.shape, 1).astype(F32)
    neg = jnp.float32(-jnp.inf)
    logits = jnp.where(lane < N_EXPERTS, logits, neg)
    m1 = jnp.max(logits, axis=-1, keepdims=True)
    i1 = jnp.min(jnp.where(logits == m1, lane, float(ROUTER_LANES)), axis=-1, keepdims=True)
    rest = jnp.where(lane == i1, neg, logits)
    m2 = jnp.max(rest, axis=-1, keepdims=True)
    i2 = jnp.min(jnp.where(rest == m2, lane, float(ROUTER_LANES)), axis=-1, keepdims=True)
    e2 = jnp.exp(m2 - m1)
    return i1, i2, 1.0 / (1.0 + e2), e2 / (1.0 + e2)


def _route_kernel(x_ref, sh_ref, sc_ref, r_ref, o_ref):
    hf = _ln_mod(x_ref[...], sh_ref[0], sc_ref[0])
    i1, i2, w1, w2 = _router_top2(hf, r_ref[...])
    lane = lax.broadcasted_iota(jnp.int32, o_ref.shape, 1)
    o_ref[...] = jnp.where(lane == 0, i1, jnp.where(lane == 1, i2, jnp.where(lane == 2, w1,
                           jnp.where(lane == 3, w2, 0.0))))


def moe_route(x2, shift, scale, router, *, seq):
    t_tok, d = x2.shape
    tm = min(seq, 1024)
    nt = seq // tm
    mod_spec = pl.BlockSpec((1, 1, d), lambda i: (i // nt, 0, 0))
    return pl.pallas_call(
        _route_kernel,
        grid=(t_tok // tm,),
        in_specs=[pl.BlockSpec((tm, d), lambda i: (i, 0)), mod_spec, mod_spec,
                  pl.BlockSpec((d, ROUTER_LANES), lambda i: (0, 0))],
        out_specs=pl.BlockSpec((tm, ROUTE_COLS), lambda i: (i, 0)),
        out_shape=jax.ShapeDtypeStruct((t_tok, ROUTE_COLS), F32),
        compiler_params=_cparams("arbitrary"),
        name="moe_route",
    )(x2, shift, scale, router)


def _dispatch_kernel(pos_ref, x_ref, sh_ref, sc_ref, xs_in_ref, xs_ref, hbuf, sem, *, tm):
    del xs_in_ref
    i = pl.program_id(0)
    n = pl.num_programs(0)
    slot = i % 2

    def row_copy(s, r, p):
        return pltpu.make_async_copy(hbuf.at[s, pl.ds(r, 1)], xs_ref.at[pl.ds(p, 1)], sem.at[s])

    def wait_slot(s):
        for _ in range(TOP_K):
            pltpu.make_async_copy(hbuf.at[s], xs_ref.at[pl.ds(0, tm)], sem.at[s]).wait()

    @pl.when(i >= 2)
    def _():
        wait_slot(slot)

    hbuf[slot] = _ln_mod(x_ref[...], sh_ref[0], sc_ref[0])

    def body(r, carry):
        base = (i * tm + r) * TOP_K
        for k in range(TOP_K):
            row_copy(slot, r, pos_ref[base + k]).start()
        return carry

    lax.fori_loop(0, tm, body, 0, unroll=8)

    @pl.when(i == n - 1)
    def _():
        wait_slot(slot)

    @pl.when((i == n - 1) & (i >= 1))
    def _():
        wait_slot(1 - slot)


def moe_dispatch(pos, x2, shift, scale, n_rows, *, seq, tm=512):
    t_tok, d = x2.shape
    tm = min(tm, seq)
    nt = seq // tm
    mod_spec = pl.BlockSpec((1, 1, d), lambda i, p: (i // nt, 0, 0))
    return pl.pallas_call(
        functools.partial(_dispatch_kernel, tm=tm),
        grid_spec=pltpu.PrefetchScalarGridSpec(
            num_scalar_prefetch=1,
            grid=(t_tok // tm,),
            in_specs=[pl.BlockSpec((tm, d), lambda i, p: (i, 0)), mod_spec, mod_spec,
                      pl.BlockSpec(memory_space=pl.ANY)],
            out_specs=pl.BlockSpec(memory_space=pl.ANY),
            scratch_shapes=[pltpu.VMEM((2, tm, d), F32), pltpu.SemaphoreType.DMA((2,))]),
        out_shape=jax.ShapeDtypeStruct((n_rows, d), F32),
        input_output_aliases={4: 0},
        compiler_params=_cparams("arbitrary"),
        name="moe_dispatch",
    )(pos, x2, shift, scale, jnp.zeros((n_rows, d), F32))


def _expert_kernel(te_ref, na_ref, xs_ref, w1_ref, w3_ref, w2_ref, o_ref, *, tk):
    active = pl.program_id(0) < na_ref[0]

    @pl.when(active)
    def _():
        xb = xs_ref[...].astype(BF16)
        for k in range(w1_ref.shape[-1] // tk):
            cols = slice(k * tk, (k + 1) * tk)
            act = _silu(_bdot(xb, w1_ref[0, :, cols])) * _bdot(xb, w3_ref[0, :, cols])
            y = _bdot(act.astype(BF16), w2_ref[0, cols, :])
            if k == 0:
                o_ref[...] = y
            else:
                o_ref[...] += y

    @pl.when(jnp.logical_not(active))
    def _():
        o_ref[...] = jnp.zeros_like(o_ref)


def moe_experts(tile_expert, n_active, xs, w1, w3, w2, *, tg, tk=512):
    n_rows, d = xs.shape
    dff = w1.shape[-1]
    wspec = lambda shape: pl.BlockSpec(shape, lambda t, te, na: (te[t], 0, 0), pipeline_mode=pl.Buffered(1))
    return pl.pallas_call(
        functools.partial(_expert_kernel, tk=tk),
        grid_spec=pltpu.PrefetchScalarGridSpec(
            num_scalar_prefetch=2,
            grid=(n_rows // tg,),
            in_specs=[pl.BlockSpec((tg, d), lambda t, te, na: (t, 0)),
                      wspec((1, d, dff)), wspec((1, d, dff)), wspec((1, dff, d))],
            out_specs=pl.BlockSpec((tg, d), lambda t, te, na: (t, 0))),
        out_shape=jax.ShapeDtypeStruct((n_rows, d), F32),
        compiler_params=_cparams("arbitrary"),
        name="moe_experts",
    )(tile_expert, n_active, xs, w1, w3, w2)


def _combine_kernel(pos_ref, x_ref, gt_ref, rt_ref, lg_ref, lb_ref, ys_ref, o_ref, ybuf, sem, *, tm, alpha):
    i = pl.program_id(0)
    n = pl.num_programs(0)
    slot = i % 2

    def issue(step, s):
        def body(r, carry):
            base = (step * tm + r) * TOP_K
            for k in range(TOP_K):
                pltpu.make_async_copy(ys_ref.at[pl.ds(pos_ref[base + k], 1)], ybuf.at[s, k, pl.ds(r, 1)],
                                      sem.at[s]).start()
            return carry
        lax.fori_loop(0, tm, body, 0, unroll=8)

    @pl.when(i == 0)
    def _():
        issue(0, 0)

    @pl.when(i + 1 < n)
    def _():
        issue(i + 1, 1 - slot)

    for k in range(TOP_K):
        pltpu.make_async_copy(ys_ref.at[pl.ds(0, tm)], ybuf.at[slot, k], sem.at[slot]).wait()
    rt = rt_ref[...]
    y = rt[:, 2:3] * ybuf[slot, 0] + rt[:, 3:4] * ybuf[slot, 1]
    o_ref[...] = _ln(alpha * x_ref[...] + gt_ref[0] * y) * lg_ref[...] + lb_ref[...]


def moe_combine(pos, x2, gate, route, ln_g, ln_b, ys, *, seq, alpha, tm=512):
    t_tok, d = x2.shape
    tm = min(tm, seq)
    nt = seq // tm
    vec = pl.BlockSpec((1, d), lambda i, p: (0, 0))
    return pl.pallas_call(
        functools.partial(_combine_kernel, tm=tm, alpha=alpha),
        grid_spec=pltpu.PrefetchScalarGridSpec(
            num_scalar_prefetch=1,
            grid=(t_tok // tm,),
            in_specs=[pl.BlockSpec((tm, d), lambda i, p: (i, 0)),
                      pl.BlockSpec((1, 1, d), lambda i, p: (i // nt, 0, 0)),
                      pl.BlockSpec((tm, ROUTE_COLS), lambda i, p: (i, 0)), vec, vec,
                      pl.BlockSpec(memory_space=pl.ANY)],
            out_specs=pl.BlockSpec((tm, d), lambda i, p: (i, 0)),
            scratch_shapes=[pltpu.VMEM((2, TOP_K, tm, d), F32), pltpu.SemaphoreType.DMA((2,))]),
        out_shape=jax.ShapeDtypeStruct((t_tok, d), F32),
        compiler_params=_cparams("arbitrary"),
        name="moe_combine",
    )(pos, x2, gate, route, ln_g, ln_b, ys)


def _dispatch_plan(route, *, tg):
    ids = route[:, 0:TOP_K].astype(jnp.int32).reshape(-1)
    n_assign = ids.shape[0]
    onehot = (ids[:, None] == jnp.arange(N_EXPERTS, dtype=jnp.int32)[None, :]).astype(jnp.int32)
    csum = jnp.cumsum(onehot, axis=0)
    counts = csum[-1]
    padded = (counts + tg - 1) // tg * tg
    ends = jnp.cumsum(padded)
    starts = ends - padded
    pos = jnp.sum(onehot * (starts[None, :] + csum - 1), axis=1).astype(jnp.int32)
    n_tiles = n_assign // tg + N_EXPERTS
    tile_start = jnp.arange(n_tiles, dtype=jnp.int32) * tg
    tile_expert = jnp.minimum(jnp.sum(tile_start[:, None] >= ends[None, :], axis=1), N_EXPERTS - 1)
    n_active = (ends[-1] // tg).reshape(1)
    return pos, tile_expert.astype(jnp.int32), n_active.astype(jnp.int32), n_tiles * tg


def ffn_moe(x2, shift, scale, gate, router, w1, w3, w2, ln_g, ln_b, *, seq, alpha, tg=1024):
    route = moe_route(x2, shift, scale, router, seq=seq)
    pos, tile_expert, n_active, n_rows = _dispatch_plan(route, tg=tg)
    xs = moe_dispatch(pos, x2, shift, scale, n_rows, seq=seq)
    ys = moe_experts(tile_expert, n_active, xs, w1, w3, w2, tg=tg)
    return moe_combine(pos, x2, gate, route, ln_g, ln_b, ys, seq=seq, alpha=alpha)


def _dft_matrices(rows, cols, n):
    f = lax.broadcasted_iota(jnp.int32, (rows, cols), 0)
    t = lax.broadcasted_iota(jnp.int32, (rows, cols), 1)
    ang = ((f * t) % n).astype(F32) * (2.0 * math.pi / n)
    return jnp.cos(ang), jnp.sin(ang)


def _twiddles(seq, width):
    ang = jnp.arange(seq // 2, dtype=F32) * (math.pi / seq)
    return jnp.broadcast_to(jnp.stack([jnp.cos(ang), jnp.sin(ang)])[:, :, None], (2, seq // 2, width))


def _position_features(seq, n_bands, width):
    t = jnp.arange(seq, dtype=F32)[:, None]
    tn = t / max(seq - 1, 1)
    bands = jnp.arange(1, n_bands + 1, dtype=F32)
    ang = t * bands * (2.0 * math.pi / seq)
    z = jnp.concatenate([tn, jnp.cos(ang), jnp.sin(ang)], axis=-1)
    return jnp.pad(z, ((0, 0), (0, width - z.shape[1])))


def _s5_discretize(a_re, a_im, log_dt, b_re, b_im):
    lam_re = jnp.minimum(a_re, S5_MAX_RE)
    lam_im = a_im
    dt = jnp.exp(log_dt)[..., None]
    mag = jnp.exp(lam_re * dt)
    ang = lam_im * dt
    abar_re = mag * jnp.cos(ang)
    abar_im = mag * jnp.sin(ang)
    den = lam_re * lam_re + lam_im * lam_im
    q_re = ((abar_re - 1.0) * lam_re + abar_im * lam_im) / den
    q_im = (abar_im * lam_re - (abar_re - 1.0) * lam_im) / den
    bb_re = q_re[..., None] * b_re - q_im[..., None] * b_im
    bb_im = q_re[..., None] * b_im + q_im[..., None] * b_re
    return abar_re, abar_im, bb_re, bb_im


def _s5_matrices(a_re, a_im, log_dt, b_re, b_im, c_re, c_im):
    abr, abi, bbr, bbi = _s5_discretize(a_re, a_im, log_dt, b_re, b_im)
    g_all, p, k = bbr.shape[1:]
    gb = S5_BLOCK_CH // S5_GROUP
    nj = g_all // gb
    eye = jnp.eye(gb, dtype=F32)

    def in_mat(bb):
        t = jnp.swapaxes(bb, -1, -2).reshape(2, nj, gb, k, p)
        return jnp.einsum("djgkp,gh->djgkhp", t, eye).reshape(2, nj, gb * k, gb * p)

    def out_mat(cc):
        t = jnp.swapaxes(cc, -1, -2).reshape(2, nj, gb, p, k)
        return jnp.einsum("djhpc,hg->djhpgc", t, eye).reshape(2, nj, gb * p, gb * k)

    bmat = jnp.concatenate([in_mat(bbr), in_mat(bbi)], axis=-1).astype(BF16)
    cmat = jnp.concatenate([out_mat(c_re), -out_mat(c_im)], axis=-2).astype(BF16)
    abar = jnp.stack([abr.reshape(2, nj, gb * p), abi.reshape(2, nj, gb * p)], axis=2)
    return bmat, cmat, abar


def kernel(x, c, ctx, c_ctx, w_mod, b_mod, w_in, hy_short_w, hy_short_b, hy_f_w1, hy_f_b1, hy_f_w2, hy_f_b2, hy_f_w3, hy_f_b3, hy_freq, hy_decay, hy_bias, w_hy_out, cf_dw_w, cf_dw_b, cf_ln_g, cf_ln_b, w_cf_out, s5_a_re, s5_a_im, s5_log_dt, s5_b_re, s5_b_im, s5_c_re, s5_c_im, s5_d, s5_w_glu, s5_b_glu, w_s5_out, w_o, ln1_g, ln1_b, ln2_g, ln2_b, ffn_w1, ffn_w3, ffn_w2, moe_router, moe_w1, moe_w3, moe_w2):
    bsz, seq, d = x.shape
    seq_c = ctx.shape[1]
    depth = w_mod.shape[0]
    alpha = (2.0 * depth) ** 0.25
    ch = D_BRANCH
    hy_cols = (HY_ORDER + 1) * ch
    cf_off = hy_cols
    s5_off = cf_off + 2 * ch
    gate_off = s5_off + ch
    n_bands = (hy_f_w1.shape[1] - 1) // 2
    nj = ch // S5_BLOCK_CH

    x2 = x.reshape(bsz * seq, d)
    xc2 = ctx.reshape(bsz * seq_c, d)
    mod_rows = 24
    cc = jnp.zeros((mod_rows, d), F32).at[:bsz].set(c).at[bsz].set(c_ctx)

    dft = {}
    feats = {}
    for s in (seq, seq_c):
        cs, ss = _dft_matrices(s // 2, s, 2 * s)
        csh, ssh = _dft_matrices(s // 2, s // 2, s)
        dft[s] = (cs, ss, csh.astype(BF16), ssh.astype(BF16), _twiddles(s, ch))
        feats[s] = _position_features(s, n_bands, 128)
    zero_state = jnp.zeros((2, nj, bsz, 2 * S5_BLOCK_ST), F32)
    row = lambda v: v[None, :]

    for l in range(depth):
        last = l == depth - 1
        mods = mod_vectors(cc, w_mod[l].astype(BF16), row(b_mod[l])).reshape(mod_rows, 6, d)
        lat = [mods[:bsz, i][:, None, :] for i in range(6)]
        cxm = [jnp.broadcast_to(mods[bsz, i][None, None, :], (bsz, 1, d)) for i in range(6)]

        w_in_b = w_in[l].astype(BF16)
        w_mix = w_in_b[:, :gate_off]
        w_s5 = w_in_b[:, s5_off:gate_off]
        w_gate = w_in_b[:, gate_off:]
        sw = jnp.concatenate([hy_short_w[l][:, i * ch:(i + 1) * ch] for i in range(HY_ORDER + 1)], axis=0)
        sb = hy_short_b[l].reshape(HY_ORDER + 1, ch)
        w1p = jnp.pad(hy_f_w1[l], ((0, 128 - hy_f_w1.shape[1]), (0, 0)))
        bmat, cmat, abar = _s5_matrices(s5_a_re[l], s5_a_im[l], s5_log_dt[l], s5_b_re[l], s5_b_im[l],
                                        s5_c_re[l], s5_c_im[l])
        merge_w = (w_gate, w_hy_out[l].astype(BF16), w_cf_out[l].astype(BF16), row(s5_d[l]),
                   s5_w_glu[l].astype(BF16), row(s5_b_glu[l]), w_s5_out[l].astype(BF16),
                   w_o[l].astype(BF16), row(ln1_g[l]), row(ln1_b[l]))

        def spectra(s):
            filt = hyena_filters(feats[s], w1p, row(hy_f_b1[l]), hy_f_w2[l], row(hy_f_b2[l]), hy_f_w3[l],
                                 row(hy_f_b3[l]), hy_freq[l], row(hy_decay[l]))
            return hyena_spectra(dft[s][0], dft[s][1], filt)

        def token_mixer(xs, s, seg, m, init, shared):
            sh1, sc1, g1 = m[0], m[1], m[2]
            p = ln_mod_proj(xs, sh1, sc1, w_mix, seq=s, shared_mod=shared)
            z_hy = hyena_mix(p, sw, sb, dft[s][2], dft[s][3], dft[s][4], *spectra(s), hy_bias[l], seq=s)
            z_cf = conformer_mix(p, cf_dw_w[l], row(cf_dw_b[l]), row(cf_ln_g[l]), row(cf_ln_b[l]), seg=seg,
                                 col_block=cf_off // ch)
            ys, fin = s5_scan(p, bmat, cmat, abar, init, seq=s, col_off=s5_off)
            x1 = merge_branches(xs, sh1, sc1, g1, z_hy, z_cf, ys, p, *merge_w, seq=s, alpha=alpha,
                                s5_col_block=s5_off // ch, shared_mod=shared)
            return x1, fin

        def channel_mixer(xs, s, m, shared):
            sh2, sc2, g2 = m[3], m[4], m[5]
            i = l // 2
            if l % 2 == 0:
                return ffn_dense(xs, sh2, sc2, g2, ffn_w1[i].astype(BF16), ffn_w3[i].astype(BF16),
                                 ffn_w2[i].astype(BF16), row(ln2_g[l]), row(ln2_b[l]), seq=s, alpha=alpha,
                                 shared_mod=shared)
            router = jnp.pad(moe_router[i], ((0, 0), (0, ROUTER_LANES - moe_router.shape[-1])))
            return ffn_moe(xs, sh2, sc2, g2, router, moe_w1[i].astype(BF16), moe_w3[i].astype(BF16),
                           moe_w2[i].astype(BF16), row(ln2_g[l]), row(ln2_b[l]), seq=s, alpha=alpha)

        if last:
            u_s5 = ln_mod_proj(xc2, cxm[0], cxm[1], w_s5, seq=seq_c, shared_mod=True)
            _, finals = s5_scan(u_s5, bmat, cmat, abar, zero_state, seq=seq_c, col_off=0)
        else:
            xc2, finals = token_mixer(xc2, seq_c, seq_c, cxm, zero_state, True)
            xc2 = channel_mixer(xc2, seq_c, cxm, True)

        x2, _ = token_mixer(x2, seq, GRID_W, lat, finals, False)
        x2 = channel_mixer(x2, seq, lat, False)
    return x2.reshape(bsz, seq, d)
```

```python
import functools
import math

import jax
import jax.numpy as jnp
from jax import lax
from jax.experimental import pallas as pl
from jax.experimental.pallas import tpu as pltpu

F32 = jnp.float32
BF16 = jnp.bfloat16
HIGHEST = lax.Precision.HIGHEST

V7X_VMEM_LIMIT_BYTES = 56 * 1024 * 1024

LN_EPS = 1e-5
D_BRANCH = 512
HY_ORDER = 2
HY_SHORT = 3
HY_ROW_BLOCK = 512
CF_KERNEL = 31
SUBLANES = 8
LANES = 128
CF_PAD = 16
CF_SHIFT_SPAN = 24
GRID_W = 64
S5_GROUP = 16
S5_STATE = 64
S5_MAX_RE = -1e-4
S5_TIME_BLOCK = 16
S5_SUB_CHUNK = 32
S5_BLOCK_CH = 128
S5_BLOCK_ST = S5_BLOCK_CH // S5_GROUP * S5_STATE
N_EXPERTS = 8
TOP_K = 2
ROUTER_LANES = 128
ROUTE_COLS = 8


def _cparams(*sem):
    return pltpu.CompilerParams(dimension_semantics=sem, vmem_limit_bytes=V7X_VMEM_LIMIT_BYTES)


def _const_spec(shape):
    nd = len(shape)
    return pl.BlockSpec(shape, lambda *_: (0,) * nd, pipeline_mode=pl.Buffered(1))


def _bdot(a, b):
    return jnp.dot(a, b, preferred_element_type=F32)


def _split_bf16(a):
    hi = a.astype(BF16)
    return hi, (a - hi.astype(F32)).astype(BF16)


def _dot3(a, b):
    return _bdot(a[0], b[0]) + (_bdot(a[1], b[0]) + _bdot(a[0], b[1]))


def _sigmoid(x):
    return 1.0 / (1.0 + jnp.exp(-x))


def _silu(x):
    return x * _sigmoid(x)


def _gelu_tanh(x):
    return 0.5 * x * (1.0 + jnp.tanh(math.sqrt(2.0 / math.pi) * (x + 0.044715 * (x * x * x))))


def _ln(x):
    mu = jnp.mean(x, axis=-1, keepdims=True)
    xc = x - mu
    var = jnp.mean(xc * xc, axis=-1, keepdims=True)
    return xc * lax.rsqrt(var + LN_EPS)


def _ln_mod(x, shift, scale):
    return _ln(x) * (1.0 + scale) + shift


def _token_tiling(t_tok, seq, cap, shared_mod):
    if shared_mod:
        return min(t_tok, cap), lambda i: 0
    tm = min(seq, cap)
    nt = seq // tm
    return tm, lambda i: i // nt


def _mod_kernel(c_ref, w_ref, b_ref, o_ref):
    s = _silu(c_ref[...])
    o_ref[...] = _bdot(s.astype(BF16), w_ref[...]) + b_ref[...]


def mod_vectors(cc, w, b, *, tn=1536):
    r, d = cc.shape
    n = w.shape[1]
    return pl.pallas_call(
        _mod_kernel,
        grid=(n // tn,),
        in_specs=[pl.BlockSpec((r, d), lambda j: (0, 0)),
                  pl.BlockSpec((d, tn), lambda j: (0, j)),
                  pl.BlockSpec((1, tn), lambda j: (0, j))],
        out_specs=pl.BlockSpec((r, tn), lambda j: (0, j)),
        out_shape=jax.ShapeDtypeStruct((r, n), F32),
        compiler_params=_cparams("arbitrary"),
        name="mod_vectors",
    )(cc, w, b)


def _proj_kernel(x_ref, sh_ref, sc_ref, w_ref, o_ref, *, tn):
    h = _ln_mod(x_ref[...], sh_ref[0], sc_ref[0]).astype(BF16)
    for j in range(o_ref.shape[1] // tn):
        cols = slice(j * tn, (j + 1) * tn)
        o_ref[:, cols] = _bdot(h, w_ref[:, cols])


def ln_mod_proj(x2, shift, scale, w, *, seq, shared_mod=False, tn=512):
    t_tok, d = x2.shape
    n = w.shape[1]
    tm, brow = _token_tiling(t_tok, seq, 1024, shared_mod)
    mod_spec = pl.BlockSpec((1, 1, d), lambda i: (brow(i), 0, 0))
    return pl.pallas_call(
        functools.partial(_proj_kernel, tn=tn),
        grid=(t_tok // tm,),
        in_specs=[pl.BlockSpec((tm, d), lambda i: (i, 0)), mod_spec, mod_spec, _const_spec((d, n))],
        out_specs=pl.BlockSpec((tm, n), lambda i: (i, 0)),
        out_shape=jax.ShapeDtypeStruct((t_tok, n), F32),
        compiler_params=_cparams("arbitrary"),
        name="ln_mod_proj",
    )(x2, shift, scale, w)


def _filter_kernel(z_ref, w1_ref, b1_ref, w2_ref, b2_ref, w3_ref, b3_ref, fr_ref, dec_ref, o_ref):
    z = z_ref[...]
    h = jnp.sin(fr_ref[0:1, :] * (jnp.dot(z, w1_ref[...], precision=HIGHEST, preferred_element_type=F32)
                                  + b1_ref[...]))
    h = jnp.sin(fr_ref[1:2, :] * (jnp.dot(h, w2_ref[...], precision=HIGHEST, preferred_element_type=F32)
                                  + b2_ref[...]))
    h = jnp.dot(h, w3_ref[...], precision=HIGHEST, preferred_element_type=F32) + b3_ref[...]
    tn = z[:, 0:1]
    o_ref[...] = h * jnp.exp(-tn * jnp.abs(dec_ref[...]))


def hyena_filters(z, w1, b1, w2, b2, w3, b3, freq, decay, *, tn=512):
    seq, zf = z.shape
    hid = w2.shape[0]
    n = w3.shape[1]
    full = lambda shape: pl.BlockSpec(shape, lambda j: (0, 0))
    return pl.pallas_call(
        _filter_kernel,
        grid=(n // tn,),
        in_specs=[full((seq, zf)), full((zf, hid)), full((1, hid)), full((hid, hid)), full((1, hid)),
                  pl.BlockSpec((hid, tn), lambda j: (0, j)), pl.BlockSpec((1, tn), lambda j: (0, j)),
                  full((2, hid)), pl.BlockSpec((1, tn), lambda j: (0, j))],
        out_specs=pl.BlockSpec((seq, tn), lambda j: (0, j)),
        out_shape=jax.ShapeDtypeStruct((seq, n), F32),
        compiler_params=_cparams("arbitrary"),
        name="hyena_filters",
    )(z, w1, b1, w2, b2, w3, b3, freq, decay)


def _spectrum_kernel(cs_ref, ss_ref, filt_ref, k1r_ref, k1i_ref, k2r_ref, k2i_ref, km_ref, *, tf, n_fft):
    ch = k1r_ref.shape[-1]
    seq = filt_ref.shape[0]
    hf = filt_ref[:, 0:ch]
    tau = lax.broadcasted_iota(jnp.int32, (seq, 1), 0)
    hb = jnp.where(tau == 0, 0.0, filt_ref[:, ch:2 * ch])
    s_cos = hf + hb
    s_sin = hb - hf
    sign = (1 - 2 * (tau & 1)).astype(F32)
    f = pl.program_id(1) * tf + lax.broadcasted_iota(jnp.int32, (tf, 1), 0)
    wgt = jnp.where(f == 0, 1.0, 2.0) / n_fft
    cs = _split_bf16(cs_ref[...])
    ss = _split_bf16(ss_ref[...])
    k1r_ref[0] = wgt * _dot3(cs, _split_bf16(s_cos))
    k1i_ref[0] = wgt * _dot3(ss, _split_bf16(s_sin))
    k2r_ref[0] = wgt * _dot3(cs, _split_bf16(sign * s_cos))
    k2i_ref[0] = -wgt * _dot3(ss, _split_bf16(sign * s_sin))
    quarter = tau & 3
    c4 = jnp.where(quarter == 0, 1.0, jnp.where(quarter == 2, -1.0, 0.0))
    s4 = jnp.where(quarter == 1, 1.0, jnp.where(quarter == 3, -1.0, 0.0))
    km_ref[0, 0:1, :] = jnp.sum(c4 * s_cos, axis=0, keepdims=True) * (2.0 / n_fft)
    km_ref[0, 1:2, :] = jnp.sum(s4 * s_sin, axis=0, keepdims=True) * (2.0 / n_fft)


def hyena_spectra(cs, ss, filt, *, tf=256):
    m, seq = cs.shape
    ch = D_BRANCH
    tf = min(tf, m)
    rows = pl.BlockSpec((tf, seq), lambda o, f: (f, 0))
    out = pl.BlockSpec((1, tf, ch), lambda o, f: (o, f, 0))
    return pl.pallas_call(
        functools.partial(_spectrum_kernel, tf=tf, n_fft=2 * seq),
        grid=(HY_ORDER, m // tf),
        in_specs=[rows, rows, pl.BlockSpec((seq, 2 * ch), lambda o, f: (0, o))],
        out_specs=[out] * 4 + [pl.BlockSpec((1, 2, ch), lambda o, f: (o, 0, 0))],
        out_shape=[jax.ShapeDtypeStruct((HY_ORDER, m, ch), F32)] * 4
                  + [jax.ShapeDtypeStruct((HY_ORDER, 2, ch), F32)],
        compiler_params=_cparams("arbitrary", "arbitrary"),
        name="hyena_spectra",
    )(cs, ss, filt)


def _hyena_kernel(*refs, nh):
    parts = [refs[p * nh:(p + 1) * nh] for p in range(HY_ORDER + 1)]
    (sw_ref, sb_ref, cs_ref, ss_ref, tw_ref, k1r_ref, k1i_ref, k2r_ref, k2i_ref, km_ref, bias_ref,
     o_ref, ze_ref, zo_ref, xe_ref, xo_ref, zeb_ref, zob_ref, g_ref, il_ref) = refs[(HY_ORDER + 1) * nh:]
    m = o_ref.shape[0] // 2
    row = lax.broadcasted_iota(jnp.int32, (m, 1), 0)
    sign = (1 - 2 * (row & 1)).astype(F32)

    def short_conv(part):
        pe = jnp.concatenate([r[pl.ds(0, m, stride=2), :] for r in parts[part]], axis=1)
        po = jnp.concatenate([r[pl.ds(1, m, stride=2), :] for r in parts[part]], axis=1)
        po_prev = jnp.where(row == 0, 0.0, pltpu.roll(po, 1, 0))
        pe_next = jnp.where(row == m - 1, 0.0, pltpu.roll(pe, m - 1, 0))
        w = sw_ref[part * HY_SHORT:(part + 1) * HY_SHORT, :]
        b = sb_ref[part:part + 1, :]
        return (w[0:1] * po_prev + w[1:2] * pe + w[2:3] * po + b,
                w[0:1] * pe + w[1:2] * po + w[2:3] * pe_next + b)

    rb = min(m, HY_ROW_BLOCK)
    blocks = [slice(r * rb, (r + 1) * rb) for r in range(m // rb)]
    ze_ref[...], zo_ref[...] = short_conv(HY_ORDER)
    for o in range(HY_ORDER):
        ze = ze_ref[...]
        zo = zo_ref[...]
        zeb_ref[...] = ze.astype(BF16)
        zob_ref[...] = zo.astype(BF16)
        e_m = jnp.sum(sign * ze, axis=0, keepdims=True)
        o_m = jnp.sum(sign * zo, axis=0, keepdims=True)
        kmr = km_ref[o, 0:1, :]
        kmi = km_ref[o, 1:2, :]
        nyq_e = e_m * kmr + o_m * kmi
        nyq_o = o_m * kmr - e_m * kmi
        bias = bias_ref[o:o + 1, :]
        xe_ref[...], xo_ref[...] = short_conv(o)
        for rows in blocks:
            cs = cs_ref[rows, :]
            ss = ss_ref[rows, :]
            ae = _bdot(cs, zeb_ref[...])
            be = _bdot(ss, zeb_ref[...])
            ao = _bdot(cs, zob_ref[...])
            bo = _bdot(ss, zob_ref[...])
            c = tw_ref[0, rows, :]
            s = tw_ref[1, rows, :]
            tr = c * ao - s * bo
            ti = -(c * bo + s * ao)
            pr, pi = ae + tr, ti - be
            qr, qi = ae - tr, -be - ti
            k1r, k1i = k1r_ref[o, rows, :], k1i_ref[o, rows, :]
            k2r, k2i = k2r_ref[o, rows, :], k2i_ref[o, rows, :]
            z1r = pr * k1r - pi * k1i
            z1i = pr * k1i + pi * k1r
            z2r = qr * k2r + qi * k2i
            z2i = qr * k2i - qi * k2r
            dr, di = z1r - z2r, z1i + z2i
            g_ref[0, rows, :] = (z1r + z2r).astype(BF16)
            g_ref[1, rows, :] = (z1i - z2i).astype(BF16)
            g_ref[2, rows, :] = (c * dr - s * di).astype(BF16)
            g_ref[3, rows, :] = (c * di + s * dr).astype(BF16)
        for rows in blocks:
            cs = cs_ref[rows, :]
            ss = ss_ref[rows, :]
            ye = _bdot(cs, g_ref[0]) - _bdot(ss, g_ref[1]) + sign[rows] * nyq_e + bias * ze_ref[rows, :]
            yo = _bdot(cs, g_ref[2]) - _bdot(ss, g_ref[3]) + sign[rows] * nyq_o + bias * zo_ref[rows, :]
            ze_ref[rows, :] = xe_ref[rows, :] * ye
            zo_ref[rows, :] = xo_ref[rows, :] * yo
    for h in range(nh):
        lanes = slice(h * LANES, (h + 1) * LANES)
        il_ref[pl.ds(0, m, stride=2), :] = ze_ref[:, lanes]
        il_ref[pl.ds(1, m, stride=2), :] = zo_ref[:, lanes]
        o_ref[:, lanes] = il_ref[...]


def hyena_mix(p, sw, sb, cs, ss, tw, k1r, k1i, k2r, k2i, km, bias, *, seq, cb=256):
    t_tok = p.shape[0]
    ch = D_BRANCH
    bsz = t_tok // seq
    m = seq // 2
    nh = cb // LANES
    col_blocks = ch // LANES
    cols = [pl.BlockSpec((seq, LANES), functools.partial(lambda j, b, c0: (b, c0 + j * nh), c0=part * col_blocks + h))
            for part in range(HY_ORDER + 1) for h in range(nh)]
    per_c = lambda rows: pl.BlockSpec((rows, cb), lambda j, b: (0, j))
    spec3 = lambda lead, rows: pl.BlockSpec((lead, rows, cb), lambda j, b: (0, 0, j),
                                            pipeline_mode=pl.Buffered(1))
    return pl.pallas_call(
        functools.partial(_hyena_kernel, nh=nh),
        grid=(ch // cb, bsz),
        in_specs=cols + [per_c(3 * HY_SHORT), per_c(3), _const_spec((m, m)), _const_spec((m, m)),
                         spec3(2, m)] + [spec3(HY_ORDER, m)] * 4 + [spec3(HY_ORDER, 2), per_c(HY_ORDER)],
        out_specs=pl.BlockSpec((seq, cb), lambda j, b: (b, j)),
        out_shape=jax.ShapeDtypeStruct((t_tok, ch), F32),
        scratch_shapes=[pltpu.VMEM((m, cb), F32)] * 4 + [pltpu.VMEM((m, cb), BF16)] * 2
                       + [pltpu.VMEM((4, m, cb), BF16), pltpu.VMEM((seq, LANES), F32)],
        compiler_params=_cparams("arbitrary", "arbitrary"),
        name="hyena_mix",
    )(*([p] * ((HY_ORDER + 1) * nh)), sw, sb, cs, ss, tw, k1r, k1i, k2r, k2i, km, bias)


def _conformer_kernel(a_ref, gl_ref, w_ref, b_ref, g_ref, be_ref, o_ref, pad_ref, shift_ref, *, seg, nseg):
    ch = a_ref.shape[-1]
    zeros = jnp.zeros((nseg, CF_PAD, ch), F32)
    pad_ref[:, 0:CF_PAD, :] = zeros
    pad_ref[:, CF_PAD + seg:, :] = zeros
    u = a_ref[...] * _sigmoid(gl_ref[...])
    pad_ref[:, CF_PAD:CF_PAD + seg, :] = u.reshape(nseg, seg, ch)
    span = seg + CF_SHIFT_SPAN
    for r in range(1, SUBLANES):
        shift_ref[r - 1] = pad_ref[:, r:r + span, :]
    half = (CF_KERNEL - 1) // 2

    def segment(s, carry):
        for c in range(ch // LANES):
            lanes = slice(c * LANES, (c + 1) * LANES)
            acc = jnp.zeros((seg, LANES), F32)
            for k in range(CF_KERNEL):
                q, r = divmod(CF_PAD - half + k, SUBLANES)
                rows = slice(q * SUBLANES, q * SUBLANES + seg)
                tap = pad_ref[s, rows, lanes] if r == 0 else shift_ref[r - 1, s, rows, lanes]
                acc = acc + w_ref[k:k + 1, lanes] * tap
            o_ref[pl.ds(pl.multiple_of(s * seg, seg), seg), lanes] = acc
        return carry

    lax.fori_loop(0, nseg, segment, 0)
    y = _ln(o_ref[...] + b_ref[...]) * g_ref[...] + be_ref[...]
    o_ref[...] = _silu(y)


def conformer_mix(p, dw_w, dw_b, ln_g, ln_b, *, seg, col_block):
    t_tok = p.shape[0]
    ch = D_BRANCH
    tm = max(seg, 512)
    nseg = tm // seg
    vec = lambda rows: pl.BlockSpec((rows, ch), lambda i: (0, 0))
    return pl.pallas_call(
        functools.partial(_conformer_kernel, seg=seg, nseg=nseg),
        grid=(t_tok // tm,),
        in_specs=[pl.BlockSpec((tm, ch), lambda i: (i, col_block)),
                  pl.BlockSpec((tm, ch), lambda i: (i, col_block + 1)),
                  vec(CF_KERNEL), vec(1), vec(1), vec(1)],
        out_specs=pl.BlockSpec((tm, ch), lambda i: (i, 0)),
        out_shape=jax.ShapeDtypeStruct((t_tok, ch), F32),
        scratch_shapes=[pltpu.VMEM((nseg, seg + 2 * CF_PAD, ch), F32),
                        pltpu.VMEM((SUBLANES - 1, nseg, seg + CF_SHIFT_SPAN, ch), F32)],
        compiler_params=_cparams("arbitrary"),
        name="conformer_mix",
    )(p, p, dw_w, dw_b, ln_g, ln_b)


def _s5_kernel(u_ref, b_ref, c_ref, a_ref, init_ref, y_ref, fin_ref, u1_ref, utm_ref, bu_ref, sb_ref,
               ytm_ref, st_ref, *, tc, bsz, reverse):
    c = pl.program_id(1)
    ns = S5_BLOCK_ST

    @pl.when(c == 0)
    def _():
        st_ref[...] = init_ref[0]

    tb = S5_TIME_BLOCK
    for th in range(tc // tb):
        for b in range(bsz):
            u1_ref[pl.ds((th * bsz + b) * tb, tb), :] = u_ref[b, th * tb:(th + 1) * tb, :]
    for th in range(tc // tb):
        for tl in range(tb):
            utm_ref[pl.ds((th * tb + tl) * bsz, bsz), :] = u1_ref[pl.ds(th * bsz * tb + tl, bsz, stride=tb), :]
    a_re = jnp.broadcast_to(a_ref[0, 0:1, :], (bsz, ns))
    a_im = jnp.broadcast_to(a_ref[0, 1:2, :], (bsz, ns))
    s_re = st_ref[:, 0:ns]
    s_im = st_ref[:, ns:2 * ns]
    sub = min(tc, S5_SUB_CHUNK)
    order = (lambda n: range(n - 1, -1, -1)) if reverse else range
    for sc in order(tc // sub):
        r0 = sc * sub * bsz
        rows = slice(r0, r0 + sub * bsz)
        bu_ref[rows, :] = _bdot(utm_ref[rows, :].astype(BF16), b_ref[0])
        for tl in order(sub):
            rr = slice(r0 + tl * bsz, r0 + (tl + 1) * bsz)
            n_re = a_re * s_re - a_im * s_im + bu_ref[rr, 0:ns]
            n_im = a_re * s_im + a_im * s_re + bu_ref[rr, ns:2 * ns]
            sb_ref[rr, 0:ns] = n_re.astype(BF16)
            sb_ref[rr, ns:2 * ns] = n_im.astype(BF16)
            s_re, s_im = n_re, n_im
        ytm_ref[rows, :] = _bdot(sb_ref[rows, :], c_ref[0])
    st_ref[:, 0:ns] = s_re
    st_ref[:, ns:2 * ns] = s_im
    for b in range(bsz):
        y_ref[b] = ytm_ref[pl.ds(b, tc, stride=bsz), :]

    @pl.when(c == pl.num_programs(1) - 1)
    def _():
        fin_ref[0] = st_ref[...]


def s5_scan(p, bmat, cmat, abar, init, *, seq, col_off, tc=128):
    t_tok, n_cols = p.shape
    ch = D_BRANCH
    bsz = t_tok // seq
    u3 = p.reshape(bsz, seq, n_cols)
    j0 = col_off // S5_BLOCK_CH
    nj = ch // S5_BLOCK_CH
    tc = min(tc, seq)
    nc = seq // tc
    rows = tc * bsz
    ns2 = 2 * S5_BLOCK_ST
    par = lambda r, cdim: pl.BlockSpec((1, r, cdim), lambda j, c: (j, 0, 0))
    ys, fins = [], []
    for d, reverse in enumerate((False, True)):
        chunk = (lambda c: nc - 1 - c) if reverse else (lambda c: c)
        y, fin = pl.pallas_call(
            functools.partial(_s5_kernel, tc=tc, bsz=bsz, reverse=reverse),
            grid=(nj, nc),
            in_specs=[pl.BlockSpec((bsz, tc, S5_BLOCK_CH),
                                   functools.partial(lambda j, c, ck: (0, ck(c), j0 + j), ck=chunk)),
                      par(S5_BLOCK_CH, ns2), par(ns2, S5_BLOCK_CH), par(2, S5_BLOCK_ST), par(bsz, ns2)],
            out_specs=[pl.BlockSpec((bsz, tc, S5_BLOCK_CH),
                                    functools.partial(lambda j, c, ck: (0, ck(c), j), ck=chunk)),
                       par(bsz, ns2)],
            out_shape=[jax.ShapeDtypeStruct((bsz, seq, ch), F32),
                       jax.ShapeDtypeStruct((nj, bsz, ns2), F32)],
            scratch_shapes=[pltpu.VMEM((rows, S5_BLOCK_CH), F32), pltpu.VMEM((rows, S5_BLOCK_CH), F32),
                            pltpu.VMEM((rows, ns2), F32), pltpu.VMEM((rows, ns2), BF16),
                            pltpu.VMEM((rows, S5_BLOCK_CH), F32), pltpu.VMEM((bsz, ns2), F32)],
            compiler_params=_cparams("arbitrary", "arbitrary"),
            name="s5_scan_bwd" if reverse else "s5_scan_fwd",
        )(u3, bmat[d], cmat[d], abar[d], init[d])
        ys.append(y.reshape(t_tok, ch))
        fins.append(fin)
    return ys, jnp.stack(fins)


def _merge_kernel(x_ref, sh_ref, sc_ref, gt_ref, zh_ref, zc_ref, ysf_ref, ysb_ref, us_ref,
                  wg_ref, why_ref, wcf_ref, ds_ref, wglu_ref, bglu_ref, ws5_ref, wo_ref,
                  lg_ref, lb_ref, o_ref, *, alpha):
    d = x_ref.shape[-1]
    x = x_ref[...]
    h = _ln_mod(x, sh_ref[0], sc_ref[0]).astype(BF16)
    gate = lambda i: _sigmoid(_bdot(h, wg_ref[:, i * d:(i + 1) * d]))
    mix = gate(0) * _bdot(zh_ref[...].astype(BF16), why_ref[...])
    mix = mix + gate(1) * _bdot(zc_ref[...].astype(BF16), wcf_ref[...])
    ys = ysf_ref[...] + ysb_ref[...] + ds_ref[...] * us_ref[...]
    g = _gelu_tanh(ys)
    y5 = g * _sigmoid(_bdot(g.astype(BF16), wglu_ref[...]) + bglu_ref[...])
    mix = mix + gate(2) * _bdot(y5.astype(BF16), ws5_ref[...])
    y = _bdot(mix.astype(BF16), wo_ref[...])
    o_ref[...] = _ln(alpha * x + gt_ref[0] * y) * lg_ref[...] + lb_ref[...]


def merge_branches(x2, shift, scale, gate, z_hy, z_cf, ys, p, w_gate, w_hy, w_cf, d_skip,
                   w_glu, b_glu, w_s5, w_o, ln_g, ln_b, *, seq, alpha, s5_col_block, shared_mod=False):
    t_tok, d = x2.shape
    ch = D_BRANCH
    tm, brow = _token_tiling(t_tok, seq, 512, shared_mod)
    mod_spec = pl.BlockSpec((1, 1, d), lambda i: (brow(i), 0, 0))
    tok = lambda n: pl.BlockSpec((tm, n), lambda i: (i, 0))
    return pl.pallas_call(
        functools.partial(_merge_kernel, alpha=alpha),
        grid=(t_tok // tm,),
        in_specs=[tok(d), mod_spec, mod_spec, mod_spec, tok(ch), tok(ch), tok(ch), tok(ch),
                  pl.BlockSpec((tm, ch), lambda i: (i, s5_col_block)),
                  _const_spec((d, 3 * d)), _const_spec((ch, d)), _const_spec((ch, d)),
                  _const_spec((1, ch)), _const_spec((ch, ch)), _const_spec((1, ch)),
                  _const_spec((ch, d)), _const_spec((d, d)), _const_spec((1, d)), _const_spec((1, d))],
        out_specs=tok(d),
        out_shape=jax.ShapeDtypeStruct((t_tok, d), F32),
        compiler_params=_cparams("arbitrary"),
        name="merge_branches",
    )(x2, shift, scale, gate, z_hy, z_cf, ys[0], ys[1], p, w_gate, w_hy, w_cf, d_skip, w_glu, b_glu,
      w_s5, w_o, ln_g, ln_b)


def _ffn_kernel(x_ref, sh_ref, sc_ref, gt_ref, w1_ref, w3_ref, w2_ref, lg_ref, lb_ref, o_ref, acc_ref,
                *, alpha, tk):
    x = x_ref[...]
    h = _ln_mod(x, sh_ref[0], sc_ref[0]).astype(BF16)
    for k in range(w1_ref.shape[1] // tk):
        cols = slice(k * tk, (k + 1) * tk)
        act = _silu(_bdot(h, w1_ref[:, cols])) * _bdot(h, w3_ref[:, cols])
        y = _bdot(act.astype(BF16), w2_ref[cols, :])
        if k == 0:
            acc_ref[...] = y
        else:
            acc_ref[...] += y
    o_ref[...] = _ln(alpha * x + gt_ref[0] * acc_ref[...]) * lg_ref[...] + lb_ref[...]


def ffn_dense(x2, shift, scale, gate, w1, w3, w2, ln_g, ln_b, *, seq, alpha, shared_mod=False, tk=256):
    t_tok, d = x2.shape
    dff = w1.shape[1]
    tm, brow = _token_tiling(t_tok, seq, 1024, shared_mod)
    mod_spec = pl.BlockSpec((1, 1, d), lambda i: (brow(i), 0, 0))
    return pl.pallas_call(
        functools.partial(_ffn_kernel, alpha=alpha, tk=tk),
        grid=(t_tok // tm,),
        in_specs=[pl.BlockSpec((tm, d), lambda i: (i, 0)), mod_spec, mod_spec, mod_spec,
                  _const_spec((d, dff)), _const_spec((d, dff)), _const_spec((dff, d)),
                  _const_spec((1, d)), _const_spec((1, d))],
        out_specs=pl.BlockSpec((tm, d), lambda i: (i, 0)),
        out_shape=jax.ShapeDtypeStruct((t_tok, d), F32),
        scratch_shapes=[pltpu.VMEM((tm, d), F32)],
        compiler_params=_cparams("arbitrary"),
        name="ffn_dense",
    )(x2, shift, scale, gate, w1, w3, w2, ln_g, ln_b)


def _router_top2(hf, router):
    logits = jnp.dot(hf, router, precision=HIGHEST, preferred_element_type=F32)
    lane = lax.broadcasted_iota(jnp.int32, logits.shape, 1).astype(F32)
    neg = jnp.float32(-jnp.inf)
    logits = jnp.where(lane < N_EXPERTS, logits, neg)
    m1 = jnp.max(logits, axis=-1, keepdims=True)
    i1 = jnp.min(jnp.where(logits == m1, lane, float(ROUTER_LANES)), axis=-1, keepdims=True)
    rest = jnp.where(lane == i1, neg, logits)
    m2 = jnp.max(rest, axis=-1, keepdims=True)
    i2 = jnp.min(jnp.where(rest == m2, lane, float(ROUTER_LANES)), axis=-1, keepdims=True)
    e2 = jnp.exp(m2 - m1)
    return i1, i2, 1.0 / (1.0 + e2), e2 / (1.0 + e2)


def _route_kernel(x_ref, sh_ref, sc_ref, r_ref, o_ref):
    hf = _ln_mod(x_ref[...], sh_ref[0], sc_ref[0])
    i1, i2, w1, w2 = _router_top2(hf, r_ref[...])
    lane = lax.broadcasted_iota(jnp.int32, o_ref.shape, 1)
    o_ref[...] = jnp.where(lane == 0, i1, jnp.where(lane == 1, i2, jnp.where(lane == 2, w1,
                           jnp.where(lane == 3, w2, 0.0))))


def moe_route(x2, shift, scale, router, *, seq):
    t_tok, d = x2.shape
    tm = min(seq, 1024)
    nt = seq // tm
    mod_spec = pl.BlockSpec((1, 1, d), lambda i: (i // nt, 0, 0))
    return pl.pallas_call(
        _route_kernel,
        grid=(t_tok // tm,),
        in_specs=[pl.BlockSpec((tm, d), lambda i: (i, 0)), mod_spec, mod_spec,
                  pl.BlockSpec((d, ROUTER_LANES), lambda i: (0, 0))],
        out_specs=pl.BlockSpec((tm, ROUTE_COLS), lambda i: (i, 0)),
        out_shape=jax.ShapeDtypeStruct((t_tok, ROUTE_COLS), F32),
        compiler_params=_cparams("arbitrary"),
        name="moe_route",
    )(x2, shift, scale, router)


def _dispatch_kernel(pos_ref, x_ref, sh_ref, sc_ref, xs_in_ref, xs_ref, hbuf, sem, *, tm):
    del xs_in_ref
    i = pl.program_id(0)
    n = pl.num_programs(0)
    slot = i % 2

    def row_copy(s, r, p):
        return pltpu.make_async_copy(hbuf.at[s, pl.ds(r, 1)], xs_ref.at[pl.ds(p, 1)], sem.at[s])

    def wait_slot(s):
        for _ in range(TOP_K):
            pltpu.make_async_copy(hbuf.at[s], xs_ref.at[pl.ds(0, tm)], sem.at[s]).wait()

    @pl.when(i >= 2)
    def _():
        wait_slot(slot)

    hbuf[slot] = _ln_mod(x_ref[...], sh_ref[0], sc_ref[0])

    def body(r, carry):
        base = (i * tm + r) * TOP_K
        for k in range(TOP_K):
            row_copy(slot, r, pos_ref[base + k]).start(priority=k)
        return carry

    lax.fori_loop(0, tm, body, 0, unroll=8)

    @pl.when(i == n - 1)
    def _():
        wait_slot(slot)

    @pl.when((i == n - 1) & (i >= 1))
    def _():
        wait_slot(1 - slot)


def moe_dispatch(pos, x2, shift, scale, n_rows, *, seq, tm=512):
    t_tok, d = x2.shape
    tm = min(tm, seq)
    nt = seq // tm
    mod_spec = pl.BlockSpec((1, 1, d), lambda i, p: (i // nt, 0, 0))
    return pl.pallas_call(
        functools.partial(_dispatch_kernel, tm=tm),
        grid_spec=pltpu.PrefetchScalarGridSpec(
            num_scalar_prefetch=1,
            grid=(t_tok // tm,),
            in_specs=[pl.BlockSpec((tm, d), lambda i, p: (i, 0)), mod_spec, mod_spec,
                      pl.BlockSpec(memory_space=pl.ANY)],
            out_specs=pl.BlockSpec(memory_space=pl.ANY),
            scratch_shapes=[pltpu.VMEM((2, tm, d), F32), pltpu.SemaphoreType.DMA((2,))]),
        out_shape=jax.ShapeDtypeStruct((n_rows, d), F32),
        input_output_aliases={4: 0},
        compiler_params=_cparams("arbitrary"),
        name="moe_dispatch",
    )(pos, x2, shift, scale, jnp.zeros((n_rows, d), F32))


def _expert_kernel(te_ref, na_ref, xs_ref, w1_ref, w3_ref, w2_ref, o_ref, *, tk):
    active = pl.program_id(0) < na_ref[0]

    @pl.when(active)
    def _():
        xb = xs_ref[...].astype(BF16)
        for k in range(w1_ref.shape[-1] // tk):
            cols = slice(k * tk, (k + 1) * tk)
            act = _silu(_bdot(xb, w1_ref[0, :, cols])) * _bdot(xb, w3_ref[0, :, cols])
            y = _bdot(act.astype(BF16), w2_ref[0, cols, :])
            if k == 0:
                o_ref[...] = y
            else:
                o_ref[...] += y

    @pl.when(jnp.logical_not(active))
    def _():
        o_ref[...] = jnp.zeros_like(o_ref)


def moe_experts(tile_expert, n_active, xs, w1, w3, w2, *, tg, tk=512):
    n_rows, d = xs.shape
    dff = w1.shape[-1]
    wspec = lambda shape: pl.BlockSpec(shape, lambda t, te, na: (te[t], 0, 0), pipeline_mode=pl.Buffered(1))
    return pl.pallas_call(
        functools.partial(_expert_kernel, tk=tk),
        grid_spec=pltpu.PrefetchScalarGridSpec(
            num_scalar_prefetch=2,
            grid=(n_rows // tg,),
            in_specs=[pl.BlockSpec((tg, d), lambda t, te, na: (t, 0)),
                      wspec((1, d, dff)), wspec((1, d, dff)), wspec((1, dff, d))],
            out_specs=pl.BlockSpec((tg, d), lambda t, te, na: (t, 0))),
        out_shape=jax.ShapeDtypeStruct((n_rows, d), F32),
        compiler_params=_cparams("arbitrary"),
        name="moe_experts",
    )(tile_expert, n_active, xs, w1, w3, w2)


def _combine_kernel(pos_ref, x_ref, gt_ref, rt_ref, lg_ref, lb_ref, ys_ref, o_ref, ybuf, sem, *, tm, alpha):
    i = pl.program_id(0)
    n = pl.num_programs(0)
    slot = i % 2

    def issue(step, s):
        def body(r, carry):
            base = (step * tm + r) * TOP_K
            for k in range(TOP_K):
                pltpu.make_async_copy(ys_ref.at[pl.ds(pos_ref[base + k], 1)], ybuf.at[s, k, pl.ds(r, 1)],
                                      sem.at[s]).start(priority=k)
            return carry
        lax.fori_loop(0, tm, body, 0, unroll=8)

    @pl.when(i == 0)
    def _():
        issue(0, 0)

    @pl.when(i + 1 < n)
    def _():
        issue(i + 1, 1 - slot)

    for k in range(TOP_K):
        pltpu.make_async_copy(ys_ref.at[pl.ds(0, tm)], ybuf.at[slot, k], sem.at[slot]).wait()
    rt = rt_ref[...]
    y = rt[:, 2:3] * ybuf[slot, 0] + rt[:, 3:4] * ybuf[slot, 1]
    o_ref[...] = _ln(alpha * x_ref[...] + gt_ref[0] * y) * lg_ref[...] + lb_ref[...]


def moe_combine(pos, x2, gate, route, ln_g, ln_b, ys, *, seq, alpha, tm=512):
    t_tok, d = x2.shape
    tm = min(tm, seq)
    nt = seq // tm
    vec = pl.BlockSpec((1, d), lambda i, p: (0, 0))
    return pl.pallas_call(
        functools.partial(_combine_kernel, tm=tm, alpha=alpha),
        grid_spec=pltpu.PrefetchScalarGridSpec(
            num_scalar_prefetch=1,
            grid=(t_tok // tm,),
            in_specs=[pl.BlockSpec((tm, d), lambda i, p: (i, 0)),
                      pl.BlockSpec((1, 1, d), lambda i, p: (i // nt, 0, 0)),
                      pl.BlockSpec((tm, ROUTE_COLS), lambda i, p: (i, 0)), vec, vec,
                      pl.BlockSpec(memory_space=pl.ANY)],
            out_specs=pl.BlockSpec((tm, d), lambda i, p: (i, 0)),
            scratch_shapes=[pltpu.VMEM((2, TOP_K, tm, d), F32), pltpu.SemaphoreType.DMA((2,))]),
        out_shape=jax.ShapeDtypeStruct((t_tok, d), F32),
        compiler_params=_cparams("arbitrary"),
        name="moe_combine",
    )(pos, x2, gate, route, ln_g, ln_b, ys)


def _dispatch_plan(route, *, tg):
    ids = route[:, 0:TOP_K].astype(jnp.int32).reshape(-1)
    n_assign = ids.shape[0]
    onehot = (ids[:, None] == jnp.arange(N_EXPERTS, dtype=jnp.int32)[None, :]).astype(jnp.int32)
    csum = jnp.cumsum(onehot, axis=0)
    counts = csum[-1]
    padded = (counts + tg - 1) // tg * tg
    ends = jnp.cumsum(padded)
    starts = ends - padded
    pos = jnp.sum(onehot * (starts[None, :] + csum - 1), axis=1).astype(jnp.int32)
    n_tiles = n_assign // tg + N_EXPERTS
    tile_start = jnp.arange(n_tiles, dtype=jnp.int32) * tg
    tile_expert = jnp.minimum(jnp.sum(tile_start[:, None] >= ends[None, :], axis=1), N_EXPERTS - 1)
    n_active = (ends[-1] // tg).reshape(1)
    return pos, tile_expert.astype(jnp.int32), n_active.astype(jnp.int32), n_tiles * tg


def ffn_moe(x2, shift, scale, gate, router, w1, w3, w2, ln_g, ln_b, *, seq, alpha, tg=1024):
    route = moe_route(x2, shift, scale, router, seq=seq)
    pos, tile_expert, n_active, n_rows = _dispatch_plan(route, tg=tg)
    xs = moe_dispatch(pos, x2, shift, scale, n_rows, seq=seq)
    ys = moe_experts(tile_expert, n_active, xs, w1, w3, w2, tg=tg)
    return moe_combine(pos, x2, gate, route, ln_g, ln_b, ys, seq=seq, alpha=alpha)


def _dft_matrices(rows, cols, n):
    f = lax.broadcasted_iota(jnp.int32, (rows, cols), 0)
    t = lax.broadcasted_iota(jnp.int32, (rows, cols), 1)
    ang = ((f * t) % n).astype(F32) * (2.0 * math.pi / n)
    return jnp.cos(ang), jnp.sin(ang)


def _twiddles(seq, width):
    ang = jnp.arange(seq // 2, dtype=F32) * (math.pi / seq)
    return jnp.broadcast_to(jnp.stack([jnp.cos(ang), jnp.sin(ang)])[:, :, None], (2, seq // 2, width))


def _position_features(seq, n_bands, width):
    t = jnp.arange(seq, dtype=F32)[:, None]
    tn = t / max(seq - 1, 1)
    bands = jnp.arange(1, n_bands + 1, dtype=F32)
    ang = t * bands * (2.0 * math.pi / seq)
    z = jnp.concatenate([tn, jnp.cos(ang), jnp.sin(ang)], axis=-1)
    return jnp.pad(z, ((0, 0), (0, width - z.shape[1])))


def _s5_discretize(a_re, a_im, log_dt, b_re, b_im):
    lam_re = jnp.minimum(a_re, S5_MAX_RE)
    lam_im = a_im
    dt = jnp.exp(log_dt)[..., None]
    mag = jnp.exp(lam_re * dt)
    ang = lam_im * dt
    abar_re = mag * jnp.cos(ang)
    abar_im = mag * jnp.sin(ang)
    den = lam_re * lam_re + lam_im * lam_im
    q_re = ((abar_re - 1.0) * lam_re + abar_im * lam_im) / den
    q_im = (abar_im * lam_re - (abar_re - 1.0) * lam_im) / den
    bb_re = q_re[..., None] * b_re - q_im[..., None] * b_im
    bb_im = q_re[..., None] * b_im + q_im[..., None] * b_re
    return abar_re, abar_im, bb_re, bb_im


def _s5_matrices(a_re, a_im, log_dt, b_re, b_im, c_re, c_im):
    abr, abi, bbr, bbi = _s5_discretize(a_re, a_im, log_dt, b_re, b_im)
    g_all, p, k = bbr.shape[1:]
    gb = S5_BLOCK_CH // S5_GROUP
    nj = g_all // gb
    eye = jnp.eye(gb, dtype=F32)

    def in_mat(bb):
        t = jnp.swapaxes(bb, -1, -2).reshape(2, nj, gb, k, p)
        return jnp.einsum("djgkp,gh->djgkhp", t, eye).reshape(2, nj, gb * k, gb * p)

    def out_mat(cc):
        t = jnp.swapaxes(cc, -1, -2).reshape(2, nj, gb, p, k)
        return jnp.einsum("djhpc,hg->djhpgc", t, eye).reshape(2, nj, gb * p, gb * k)

    bmat = jnp.concatenate([in_mat(bbr), in_mat(bbi)], axis=-1).astype(BF16)
    cmat = jnp.concatenate([out_mat(c_re), -out_mat(c_im)], axis=-2).astype(BF16)
    abar = jnp.stack([abr.reshape(2, nj, gb * p), abi.reshape(2, nj, gb * p)], axis=2)
    return bmat, cmat, abar


def kernel(x, c, ctx, c_ctx, w_mod, b_mod, w_in, hy_short_w, hy_short_b, hy_f_w1, hy_f_b1, hy_f_w2, hy_f_b2, hy_f_w3, hy_f_b3, hy_freq, hy_decay, hy_bias, w_hy_out, cf_dw_w, cf_dw_b, cf_ln_g, cf_ln_b, w_cf_out, s5_a_re, s5_a_im, s5_log_dt, s5_b_re, s5_b_im, s5_c_re, s5_c_im, s5_d, s5_w_glu, s5_b_glu, w_s5_out, w_o, ln1_g, ln1_b, ln2_g, ln2_b, ffn_w1, ffn_w3, ffn_w2, moe_router, moe_w1, moe_w3, moe_w2):
    bsz, seq, d = x.shape
    seq_c = ctx.shape[1]
    depth = w_mod.shape[0]
    alpha = (2.0 * depth) ** 0.25
    ch = D_BRANCH
    hy_cols = (HY_ORDER + 1) * ch
    cf_off = hy_cols
    s5_off = cf_off + 2 * ch
    gate_off = s5_off + ch
    n_bands = (hy_f_w1.shape[1] - 1) // 2
    nj = ch // S5_BLOCK_CH

    x2 = x.reshape(bsz * seq, d)
    xc2 = ctx.reshape(bsz * seq_c, d)
    mod_rows = 24
    cc = jnp.zeros((mod_rows, d), F32).at[:bsz].set(c).at[bsz].set(c_ctx)

    dft = {}
    feats = {}
    for s in (seq, seq_c):
        cs, ss = _dft_matrices(s // 2, s, 2 * s)
        csh, ssh = _dft_matrices(s // 2, s // 2, s)
        dft[s] = (cs, ss, csh.astype(BF16), ssh.astype(BF16), _twiddles(s, ch))
        feats[s] = _position_features(s, n_bands, 128)
    zero_state = jnp.zeros((2, nj, bsz, 2 * S5_BLOCK_ST), F32)
    row = lambda v: v[None, :]

    for l in range(depth):
        last = l == depth - 1
        mods = mod_vectors(cc, w_mod[l].astype(BF16), row(b_mod[l])).reshape(mod_rows, 6, d)
        lat = [mods[:bsz, i][:, None, :] for i in range(6)]
        cxm = [jnp.broadcast_to(mods[bsz, i][None, None, :], (bsz, 1, d)) for i in range(6)]

        w_in_b = w_in[l].astype(BF16)
        w_mix = w_in_b[:, :gate_off]
        w_s5 = w_in_b[:, s5_off:gate_off]
        w_gate = w_in_b[:, gate_off:]
        sw = jnp.concatenate([hy_short_w[l][:, i * ch:(i + 1) * ch] for i in range(HY_ORDER + 1)], axis=0)
        sb = hy_short_b[l].reshape(HY_ORDER + 1, ch)
        w1p = jnp.pad(hy_f_w1[l], ((0, 128 - hy_f_w1.shape[1]), (0, 0)))
        bmat, cmat, abar = _s5_matrices(s5_a_re[l], s5_a_im[l], s5_log_dt[l], s5_b_re[l], s5_b_im[l],
                                        s5_c_re[l], s5_c_im[l])
        merge_w = (w_gate, w_hy_out[l].astype(BF16), w_cf_out[l].astype(BF16), row(s5_d[l]),
                   s5_w_glu[l].astype(BF16), row(s5_b_glu[l]), w_s5_out[l].astype(BF16),
                   w_o[l].astype(BF16), row(ln1_g[l]), row(ln1_b[l]))

        def spectra(s):
            filt = hyena_filters(feats[s], w1p, row(hy_f_b1[l]), hy_f_w2[l], row(hy_f_b2[l]), hy_f_w3[l],
                                 row(hy_f_b3[l]), hy_freq[l], row(hy_decay[l]))
            return hyena_spectra(dft[s][0], dft[s][1], filt)

        def token_mixer(xs, s, seg, m, init, shared):
            sh1, sc1, g1 = m[0], m[1], m[2]
            p = ln_mod_proj(xs, sh1, sc1, w_mix, seq=s, shared_mod=shared)
            z_hy = hyena_mix(p, sw, sb, dft[s][2], dft[s][3], dft[s][4], *spectra(s), hy_bias[l], seq=s)
            z_cf = conformer_mix(p, cf_dw_w[l], row(cf_dw_b[l]), row(cf_ln_g[l]), row(cf_ln_b[l]), seg=seg,
                                 col_block=cf_off // ch)
            ys, fin = s5_scan(p, bmat, cmat, abar, init, seq=s, col_off=s5_off)
            x1 = merge_branches(xs, sh1, sc1, g1, z_hy, z_cf, ys, p, *merge_w, seq=s, alpha=alpha,
                                s5_col_block=s5_off // ch, shared_mod=shared)
            return x1, fin

        def channel_mixer(xs, s, m, shared):
            sh2, sc2, g2 = m[3], m[4], m[5]
            i = l // 2
            if l % 2 == 0:
                return ffn_dense(xs, sh2, sc2, g2, ffn_w1[i].astype(BF16), ffn_w3[i].astype(BF16),
                                 ffn_w2[i].astype(BF16), row(ln2_g[l]), row(ln2_b[l]), seq=s, alpha=alpha,
                                 shared_mod=shared)
            router = jnp.pad(moe_router[i], ((0, 0), (0, ROUTER_LANES - moe_router.shape[-1])))
            return ffn_moe(xs, sh2, sc2, g2, router, moe_w1[i].astype(BF16), moe_w3[i].astype(BF16),
                           moe_w2[i].astype(BF16), row(ln2_g[l]), row(ln2_b[l]), seq=s, alpha=alpha)

        if last:
            u_s5 = ln_mod_proj(xc2, cxm[0], cxm[1], w_s5, seq=seq_c, shared_mod=True)
            _, finals = s5_scan(u_s5, bmat, cmat, abar, zero_state, seq=seq_c, col_off=0)
        else:
            xc2, finals = token_mixer(xc2, seq_c, seq_c, cxm, zero_state, True)
            xc2 = channel_mixer(xc2, seq_c, cxm, True)

        x2, _ = token_mixer(x2, seq, GRID_W, lat, finals, False)
        x2 = channel_mixer(x2, seq, lat, False)
    return x2.reshape(bsz, seq, d)
```

```python
import functools
import math

import jax
import jax.numpy as jnp
from jax import lax
from jax.experimental import pallas as pl
from jax.experimental.pallas import tpu as pltpu

F32 = jnp.float32
BF16 = jnp.bfloat16
HIGHEST = lax.Precision.HIGHEST

V7X_VMEM_LIMIT_BYTES = 56 * 1024 * 1024

LN_EPS = 1e-5
D_BRANCH = 512
HY_ORDER = 2
HY_SHORT = 3
HY_ROW_BLOCK = 512
CF_KERNEL = 31
SUBLANES = 8
LANES = 128
CF_PAD = 16
CF_SHIFT_SPAN = 24
GRID_W = 64
S5_GROUP = 16
S5_STATE = 64
S5_MAX_RE = -1e-4
S5_TIME_BLOCK = 16
S5_SUB_CHUNK = 32
S5_BLOCK_CH = 128
S5_BLOCK_ST = S5_BLOCK_CH // S5_GROUP * S5_STATE
N_EXPERTS = 8
TOP_K = 2
ROUTER_LANES = 128
ROUTE_COLS = 8


def _cparams(*sem):
    return pltpu.CompilerParams(dimension_semantics=sem, vmem_limit_bytes=V7X_VMEM_LIMIT_BYTES)


def _const_spec(shape):
    nd = len(shape)
    return pl.BlockSpec(shape, lambda *_: (0,) * nd, pipeline_mode=pl.Buffered(1))


def _bdot(a, b):
    return jnp.dot(a, b, preferred_element_type=F32)


def _split_bf16(a):
    hi = a.astype(BF16)
    return hi, (a - hi.astype(F32)).astype(BF16)


def _dot3(a, b):
    return _bdot(a[0], b[0]) + (_bdot(a[1], b[0]) + _bdot(a[0], b[1]))


def _sigmoid(x):
    return 1.0 / (1.0 + jnp.exp(-x))


def _silu(x):
    return x * _sigmoid(x)


def _gelu_tanh(x):
    return 0.5 * x * (1.0 + jnp.tanh(math.sqrt(2.0 / math.pi) * (x + 0.044715 * (x * x * x))))


def _ln(x):
    mu = jnp.mean(x, axis=-1, keepdims=True)
    xc = x - mu
    var = jnp.mean(xc * xc, axis=-1, keepdims=True)
    return xc * lax.rsqrt(var + LN_EPS)


def _ln_mod(x, shift, scale):
    return _ln(x) * (1.0 + scale) + shift


def _token_tiling(t_tok, seq, cap, shared_mod):
    if shared_mod:
        return min(t_tok, cap), lambda i: 0
    tm = min(seq, cap)
    nt = seq // tm
    return tm, lambda i: i // nt


def _mod_kernel(c_ref, w_ref, b_ref, o_ref):
    s = _silu(c_ref[...])
    o_ref[...] = _bdot(s.astype(BF16), w_ref[...]) + b_ref[...]


def mod_vectors(cc, w, b, *, tn=1536):
    r, d = cc.shape
    n = w.shape[1]
    return pl.pallas_call(
        _mod_kernel,
        grid=(n // tn,),
        in_specs=[pl.BlockSpec((r, d), lambda j: (0, 0)),
                  pl.BlockSpec((d, tn), lambda j: (0, j)),
                  pl.BlockSpec((1, tn), lambda j: (0, j))],
        out_specs=pl.BlockSpec((r, tn), lambda j: (0, j)),
        out_shape=jax.ShapeDtypeStruct((r, n), F32),
        compiler_params=_cparams("arbitrary"),
        name="mod_vectors",
    )(cc, w, b)


def _proj_kernel(x_ref, sh_ref, sc_ref, w_ref, o_ref, *, tn):
    h = _ln_mod(x_ref[...], sh_ref[0], sc_ref[0]).astype(BF16)
    for j in range(o_ref.shape[1] // tn):
        cols = slice(j * tn, (j + 1) * tn)
        o_ref[:, cols] = _bdot(h, w_ref[:, cols])


def ln_mod_proj(x2, shift, scale, w, *, seq, shared_mod=False, tn=512):
    t_tok, d = x2.shape
    n = w.shape[1]
    tm, brow = _token_tiling(t_tok, seq, 1024, shared_mod)
    mod_spec = pl.BlockSpec((1, 1, d), lambda i: (brow(i), 0, 0))
    return pl.pallas_call(
        functools.partial(_proj_kernel, tn=tn),
        grid=(t_tok // tm,),
        in_specs=[pl.BlockSpec((tm, d), lambda i: (i, 0)), mod_spec, mod_spec, _const_spec((d, n))],
        out_specs=pl.BlockSpec((tm, n), lambda i: (i, 0)),
        out_shape=jax.ShapeDtypeStruct((t_tok, n), F32),
        compiler_params=_cparams("arbitrary"),
        name="ln_mod_proj",
    )(x2, shift, scale, w)


def _filter_kernel(z_ref, w1_ref, b1_ref, w2_ref, b2_ref, w3_ref, b3_ref, fr_ref, dec_ref, o_ref):
    z = z_ref[...]
    h = jnp.sin(fr_ref[0:1, :] * (jnp.dot(z, w1_ref[...], precision=HIGHEST, preferred_element_type=F32)
                                  + b1_ref[...]))
    h = jnp.sin(fr_ref[1:2, :] * (jnp.dot(h, w2_ref[...], precision=HIGHEST, preferred_element_type=F32)
                                  + b2_ref[...]))
    h = jnp.dot(h, w3_ref[...], precision=HIGHEST, preferred_element_type=F32) + b3_ref[...]
    tn = z[:, 0:1]
    o_ref[...] = h * jnp.exp(-tn * jnp.abs(dec_ref[...]))


def hyena_filters(z, w1, b1, w2, b2, w3, b3, freq, decay, *, tn=512):
    seq, zf = z.shape
    hid = w2.shape[0]
    n = w3.shape[1]
    full = lambda shape: pl.BlockSpec(shape, lambda j: (0, 0))
    return pl.pallas_call(
        _filter_kernel,
        grid=(n // tn,),
        in_specs=[full((seq, zf)), full((zf, hid)), full((1, hid)), full((hid, hid)), full((1, hid)),
                  pl.BlockSpec((hid, tn), lambda j: (0, j)), pl.BlockSpec((1, tn), lambda j: (0, j)),
                  full((2, hid)), pl.BlockSpec((1, tn), lambda j: (0, j))],
        out_specs=pl.BlockSpec((seq, tn), lambda j: (0, j)),
        out_shape=jax.ShapeDtypeStruct((seq, n), F32),
        compiler_params=_cparams("arbitrary"),
        name="hyena_filters",
    )(z, w1, b1, w2, b2, w3, b3, freq, decay)


def _spectrum_kernel(cs_ref, ss_ref, filt_ref, k1r_ref, k1i_ref, k2r_ref, k2i_ref, km_ref, *, tf, n_fft):
    ch = k1r_ref.shape[-1]
    seq = filt_ref.shape[0]
    hf = filt_ref[:, 0:ch]
    tau = lax.broadcasted_iota(jnp.int32, (seq, 1), 0)
    hb = jnp.where(tau == 0, 0.0, filt_ref[:, ch:2 * ch])
    s_cos = hf + hb
    s_sin = hb - hf
    sign = (1 - 2 * (tau & 1)).astype(F32)
    f = pl.program_id(1) * tf + lax.broadcasted_iota(jnp.int32, (tf, 1), 0)
    wgt = jnp.where(f == 0, 1.0, 2.0) / n_fft
    cs = _split_bf16(cs_ref[...])
    ss = _split_bf16(ss_ref[...])
    k1r_ref[0] = wgt * _dot3(cs, _split_bf16(s_cos))
    k1i_ref[0] = wgt * _dot3(ss, _split_bf16(s_sin))
    k2r_ref[0] = wgt * _dot3(cs, _split_bf16(sign * s_cos))
    k2i_ref[0] = -wgt * _dot3(ss, _split_bf16(sign * s_sin))
    quarter = tau & 3
    c4 = jnp.where(quarter == 0, 1.0, jnp.where(quarter == 2, -1.0, 0.0))
    s4 = jnp.where(quarter == 1, 1.0, jnp.where(quarter == 3, -1.0, 0.0))
    km_ref[0, 0:1, :] = jnp.sum(c4 * s_cos, axis=0, keepdims=True) * (2.0 / n_fft)
    km_ref[0, 1:2, :] = jnp.sum(s4 * s_sin, axis=0, keepdims=True) * (2.0 / n_fft)


def hyena_spectra(cs, ss, filt, *, tf=256):
    m, seq = cs.shape
    ch = D_BRANCH
    tf = min(tf, m)
    rows = pl.BlockSpec((tf, seq), lambda o, f: (f, 0))
    out = pl.BlockSpec((1, tf, ch), lambda o, f: (o, f, 0))
    return pl.pallas_call(
        functools.partial(_spectrum_kernel, tf=tf, n_fft=2 * seq),
        grid=(HY_ORDER, m // tf),
        in_specs=[rows, rows, pl.BlockSpec((seq, 2 * ch), lambda o, f: (0, o))],
        out_specs=[out] * 4 + [pl.BlockSpec((1, 2, ch), lambda o, f: (o, 0, 0))],
        out_shape=[jax.ShapeDtypeStruct((HY_ORDER, m, ch), F32)] * 4
                  + [jax.ShapeDtypeStruct((HY_ORDER, 2, ch), F32)],
        compiler_params=_cparams("arbitrary", "arbitrary"),
        name="hyena_spectra",
    )(cs, ss, filt)


def _hyena_kernel(*refs, nh):
    parts = [refs[p * nh:(p + 1) * nh] for p in range(HY_ORDER + 1)]
    (sw_ref, sb_ref, cs_ref, ss_ref, tw_ref, k1r_ref, k1i_ref, k2r_ref, k2i_ref, km_ref, bias_ref,
     o_ref, ze_ref, zo_ref, xe_ref, xo_ref, zeb_ref, zob_ref, g_ref, il_ref) = refs[(HY_ORDER + 1) * nh:]
    m = o_ref.shape[0] // 2
    row = lax.broadcasted_iota(jnp.int32, (m, 1), 0)
    sign = (1 - 2 * (row & 1)).astype(F32)

    def short_conv(part):
        pe = jnp.concatenate([r[pl.ds(0, m, stride=2), :] for r in parts[part]], axis=1)
        po = jnp.concatenate([r[pl.ds(1, m, stride=2), :] for r in parts[part]], axis=1)
        po_prev = jnp.where(row == 0, 0.0, pltpu.roll(po, 1, 0))
        pe_next = jnp.where(row == m - 1, 0.0, pltpu.roll(pe, m - 1, 0))
        w = sw_ref[part * HY_SHORT:(part + 1) * HY_SHORT, :]
        b = sb_ref[part:part + 1, :]
        return (w[0:1] * po_prev + w[1:2] * pe + w[2:3] * po + b,
                w[0:1] * pe + w[1:2] * po + w[2:3] * pe_next + b)

    rb = min(m, HY_ROW_BLOCK)
    blocks = [slice(r * rb, (r + 1) * rb) for r in range(m // rb)]
    ze_ref[...], zo_ref[...] = short_conv(HY_ORDER)
    for o in range(HY_ORDER):
        ze = ze_ref[...]
        zo = zo_ref[...]
        zeb_ref[...] = ze.astype(BF16)
        zob_ref[...] = zo.astype(BF16)
        e_m = jnp.sum(sign * ze, axis=0, keepdims=True)
        o_m = jnp.sum(sign * zo, axis=0, keepdims=True)
        kmr = km_ref[o, 0:1, :]
        kmi = km_ref[o, 1:2, :]
        nyq_e = e_m * kmr + o_m * kmi
        nyq_o = o_m * kmr - e_m * kmi
        bias = bias_ref[o:o + 1, :]
        xe_ref[...], xo_ref[...] = short_conv(o)
        for rows in blocks:
            cs = cs_ref[rows, :]
            ss = ss_ref[rows, :]
            ae = _bdot(cs, zeb_ref[...])
            be = _bdot(ss, zeb_ref[...])
            ao = _bdot(cs, zob_ref[...])
            bo = _bdot(ss, zob_ref[...])
            c = tw_ref[0, rows, :]
            s = tw_ref[1, rows, :]
            tr = c * ao - s * bo
            ti = -(c * bo + s * ao)
            pr, pi = ae + tr, ti - be
            qr, qi = ae - tr, -be - ti
            k1r, k1i = k1r_ref[o, rows, :], k1i_ref[o, rows, :]
            k2r, k2i = k2r_ref[o, rows, :], k2i_ref[o, rows, :]
            z1r = pr * k1r - pi * k1i
            z1i = pr * k1i + pi * k1r
            z2r = qr * k2r + qi * k2i
            z2i = qr * k2i - qi * k2r
            dr, di = z1r - z2r, z1i + z2i
            g_ref[0, rows, :] = (z1r + z2r).astype(BF16)
            g_ref[1, rows, :] = (z1i - z2i).astype(BF16)
            g_ref[2, rows, :] = (c * dr - s * di).astype(BF16)
            g_ref[3, rows, :] = (c * di + s * dr).astype(BF16)
        for rows in blocks:
            cs = cs_ref[rows, :]
            ss = ss_ref[rows, :]
            ye = _bdot(cs, g_ref[0]) - _bdot(ss, g_ref[1]) + sign[rows] * nyq_e + bias * ze_ref[rows, :]
            yo = _bdot(cs, g_ref[2]) - _bdot(ss, g_ref[3]) + sign[rows] * nyq_o + bias * zo_ref[rows, :]
            ze_ref[rows, :] = xe_ref[rows, :] * ye
            zo_ref[rows, :] = xo_ref[rows, :] * yo
    for h in range(nh):
        lanes = slice(h * LANES, (h + 1) * LANES)
        il_ref[pl.ds(0, m, stride=2), :] = ze_ref[:, lanes]
        il_ref[pl.ds(1, m, stride=2), :] = zo_ref[:, lanes]
        o_ref[:, lanes] = il_ref[...]


def hyena_mix(p, sw, sb, cs, ss, tw, k1r, k1i, k2r, k2i, km, bias, *, seq, cb=256):
    t_tok = p.shape[0]
    ch = D_BRANCH
    bsz = t_tok // seq
    m = seq // 2
    nh = cb // LANES
    col_blocks = ch // LANES
    cols = [pl.BlockSpec((seq, LANES), functools.partial(lambda j, b, c0: (b, c0 + j * nh), c0=part * col_blocks + h))
            for part in range(HY_ORDER + 1) for h in range(nh)]
    per_c = lambda rows: pl.BlockSpec((rows, cb), lambda j, b: (0, j))
    spec3 = lambda lead, rows: pl.BlockSpec((lead, rows, cb), lambda j, b: (0, 0, j),
                                            pipeline_mode=pl.Buffered(1))
    return pl.pallas_call(
        functools.partial(_hyena_kernel, nh=nh),
        grid=(ch // cb, bsz),
        in_specs=cols + [per_c(3 * HY_SHORT), per_c(3), _const_spec((m, m)), _const_spec((m, m)),
                         spec3(2, m)] + [spec3(HY_ORDER, m)] * 4 + [spec3(HY_ORDER, 2), per_c(HY_ORDER)],
        out_specs=pl.BlockSpec((seq, cb), lambda j, b: (b, j)),
        out_shape=jax.ShapeDtypeStruct((t_tok, ch), F32),
        scratch_shapes=[pltpu.VMEM((m, cb), F32)] * 4 + [pltpu.VMEM((m, cb), BF16)] * 2
                       + [pltpu.VMEM((4, m, cb), BF16), pltpu.VMEM((seq, LANES), F32)],
        compiler_params=_cparams("arbitrary", "arbitrary"),
        name="hyena_mix",
    )(*([p] * ((HY_ORDER + 1) * nh)), sw, sb, cs, ss, tw, k1r, k1i, k2r, k2i, km, bias)


def _conformer_kernel(a_ref, gl_ref, w_ref, b_ref, g_ref, be_ref, o_ref, pad_ref, shift_ref, *, seg, nseg):
    ch = a_ref.shape[-1]
    zeros = jnp.zeros((nseg, CF_PAD, ch), F32)
    pad_ref[:, 0:CF_PAD, :] = zeros
    pad_ref[:, CF_PAD + seg:, :] = zeros
    u = a_ref[...] * _sigmoid(gl_ref[...])
    pad_ref[:, CF_PAD:CF_PAD + seg, :] = u.reshape(nseg, seg, ch)
    span = seg + CF_SHIFT_SPAN
    for r in range(1, SUBLANES):
        shift_ref[r - 1] = pad_ref[:, r:r + span, :]
    half = (CF_KERNEL - 1) // 2

    def segment(s, carry):
        for c in range(ch // LANES):
            lanes = slice(c * LANES, (c + 1) * LANES)
            acc = jnp.zeros((seg, LANES), F32)
            for k in range(CF_KERNEL):
                q, r = divmod(CF_PAD - half + k, SUBLANES)
                rows = slice(q * SUBLANES, q * SUBLANES + seg)
                tap = pad_ref[s, rows, lanes] if r == 0 else shift_ref[r - 1, s, rows, lanes]
                acc = acc + w_ref[k:k + 1, lanes] * tap
            o_ref[pl.ds(pl.multiple_of(s * seg, seg), seg), lanes] = acc
        return carry

    lax.fori_loop(0, nseg, segment, 0)
    y = _ln(o_ref[...] + b_ref[...]) * g_ref[...] + be_ref[...]
    o_ref[...] = _silu(y)


def conformer_mix(p, dw_w, dw_b, ln_g, ln_b, *, seg, col_block):
    t_tok = p.shape[0]
    ch = D_BRANCH
    tm = max(seg, 512)
    nseg = tm // seg
    vec = lambda rows: pl.BlockSpec((rows, ch), lambda i: (0, 0))
    return pl.pallas_call(
        functools.partial(_conformer_kernel, seg=seg, nseg=nseg),
        grid=(t_tok // tm,),
        in_specs=[pl.BlockSpec((tm, ch), lambda i: (i, col_block)),
                  pl.BlockSpec((tm, ch), lambda i: (i, col_block + 1)),
                  vec(CF_KERNEL), vec(1), vec(1), vec(1)],
        out_specs=pl.BlockSpec((tm, ch), lambda i: (i, 0)),
        out_shape=jax.ShapeDtypeStruct((t_tok, ch), F32),
        scratch_shapes=[pltpu.VMEM((nseg, seg + 2 * CF_PAD, ch), F32),
                        pltpu.VMEM((SUBLANES - 1, nseg, seg + CF_SHIFT_SPAN, ch), F32)],
        compiler_params=_cparams("arbitrary"),
        name="conformer_mix",
    )(p, p, dw_w, dw_b, ln_g, ln_b)


def _s5_kernel(u_ref, b_ref, c_ref, a_ref, init_ref, y_ref, fin_ref, u1_ref, utm_ref, bu_ref, sb_ref,
               ytm_ref, st_ref, *, tc, bsz, reverse):
    c = pl.program_id(1)
    ns = S5_BLOCK_ST

    @pl.when(c == 0)
    def _():
        st_ref[...] = init_ref[0]

    tb = S5_TIME_BLOCK
    for th in range(tc // tb):
        for b in range(bsz):
            u1_ref[pl.ds((th * bsz + b) * tb, tb), :] = u_ref[b, th * tb:(th + 1) * tb, :]
    for th in range(tc // tb):
        for tl in range(tb):
            utm_ref[pl.ds((th * tb + tl) * bsz, bsz), :] = u1_ref[pl.ds(th * bsz * tb + tl, bsz, stride=tb), :]
    a_re = jnp.broadcast_to(a_ref[0, 0:1, :], (bsz, ns))
    a_im = jnp.broadcast_to(a_ref[0, 1:2, :], (bsz, ns))
    s_re = st_ref[:, 0:ns]
    s_im = st_ref[:, ns:2 * ns]
    sub = min(tc, S5_SUB_CHUNK)
    order = (lambda n: range(n - 1, -1, -1)) if reverse else range
    for sc in order(tc // sub):
        r0 = sc * sub * bsz
        rows = slice(r0, r0 + sub * bsz)
        bu_ref[rows, :] = _bdot(utm_ref[rows, :].astype(BF16), b_ref[0])
        for tl in order(sub):
            rr = slice(r0 + tl * bsz, r0 + (tl + 1) * bsz)
            n_re = a_re * s_re - a_im * s_im + bu_ref[rr, 0:ns]
            n_im = a_re * s_im + a_im * s_re + bu_ref[rr, ns:2 * ns]
            sb_ref[rr, 0:ns] = n_re.astype(BF16)
            sb_ref[rr, ns:2 * ns] = n_im.astype(BF16)
            s_re, s_im = n_re, n_im
        ytm_ref[rows, :] = _bdot(sb_ref[rows, :], c_ref[0])
    st_ref[:, 0:ns] = s_re
    st_ref[:, ns:2 * ns] = s_im
    for b in range(bsz):
        y_ref[b] = ytm_ref[pl.ds(b, tc, stride=bsz), :]

    @pl.when(c == pl.num_programs(1) - 1)
    def _():
        fin_ref[0] = st_ref[...]


def s5_scan(p, bmat, cmat, abar, init, *, seq, col_off, tc=128):
    t_tok, n_cols = p.shape
    ch = D_BRANCH
    bsz = t_tok // seq
    u3 = p.reshape(bsz, seq, n_cols)
    j0 = col_off // S5_BLOCK_CH
    nj = ch // S5_BLOCK_CH
    tc = min(tc, seq)
    nc = seq // tc
    rows = tc * bsz
    ns2 = 2 * S5_BLOCK_ST
    par = lambda r, cdim: pl.BlockSpec((1, r, cdim), lambda j, c: (j, 0, 0))
    ys, fins = [], []
    for d, reverse in enumerate((False, True)):
        chunk = (lambda c: nc - 1 - c) if reverse else (lambda c: c)
        y, fin = pl.pallas_call(
            functools.partial(_s5_kernel, tc=tc, bsz=bsz, reverse=reverse),
            grid=(nj, nc),
            in_specs=[pl.BlockSpec((bsz, tc, S5_BLOCK_CH),
                                   functools.partial(lambda j, c, ck: (0, ck(c), j0 + j), ck=chunk)),
                      par(S5_BLOCK_CH, ns2), par(ns2, S5_BLOCK_CH), par(2, S5_BLOCK_ST), par(bsz, ns2)],
            out_specs=[pl.BlockSpec((bsz, tc, S5_BLOCK_CH),
                                    functools.partial(lambda j, c, ck: (0, ck(c), j), ck=chunk)),
                       par(bsz, ns2)],
            out_shape=[jax.ShapeDtypeStruct((bsz, seq, ch), F32),
                       jax.ShapeDtypeStruct((nj, bsz, ns2), F32)],
            scratch_shapes=[pltpu.VMEM((rows, S5_BLOCK_CH), F32), pltpu.VMEM((rows, S5_BLOCK_CH), F32),
                            pltpu.VMEM((rows, ns2), F32), pltpu.VMEM((rows, ns2), BF16),
                            pltpu.VMEM((rows, S5_BLOCK_CH), F32), pltpu.VMEM((bsz, ns2), F32)],
            compiler_params=_cparams("arbitrary", "arbitrary"),
            name="s5_scan_bwd" if reverse else "s5_scan_fwd",
        )(u3, bmat[d], cmat[d], abar[d], init[d])
        ys.append(y.reshape(t_tok, ch))
        fins.append(fin)
    return ys, jnp.stack(fins)


def _merge_kernel(x_ref, sh_ref, sc_ref, gt_ref, zh_ref, zc_ref, ysf_ref, ysb_ref, us_ref,
                  wg_ref, why_ref, wcf_ref, ds_ref, wglu_ref, bglu_ref, ws5_ref, wo_ref,
                  lg_ref, lb_ref, o_ref, *, alpha):
    d = x_ref.shape[-1]
    x = x_ref[...]
    h = _ln_mod(x, sh_ref[0], sc_ref[0]).astype(BF16)
    gate = lambda i: _sigmoid(_bdot(h, wg_ref[:, i * d:(i + 1) * d]))
    mix = gate(0) * _bdot(zh_ref[...].astype(BF16), why_ref[...])
    mix = mix + gate(1) * _bdot(zc_ref[...].astype(BF16), wcf_ref[...])
    ys = ysf_ref[...] + ysb_ref[...] + ds_ref[...] * us_ref[...]
    g = _gelu_tanh(ys)
    y5 = g * _sigmoid(_bdot(g.astype(BF16), wglu_ref[...]) + bglu_ref[...])
    mix = mix + gate(2) * _bdot(y5.astype(BF16), ws5_ref[...])
    y = _bdot(mix.astype(BF16), wo_ref[...])
    o_ref[...] = _ln(alpha * x + gt_ref[0] * y) * lg_ref[...] + lb_ref[...]


def merge_branches(x2, shift, scale, gate, z_hy, z_cf, ys, p, w_gate, w_hy, w_cf, d_skip,
                   w_glu, b_glu, w_s5, w_o, ln_g, ln_b, *, seq, alpha, s5_col_block, shared_mod=False):
    t_tok, d = x2.shape
    ch = D_BRANCH
    tm, brow = _token_tiling(t_tok, seq, 512, shared_mod)
    mod_spec = pl.BlockSpec((1, 1, d), lambda i: (brow(i), 0, 0))
    tok = lambda n: pl.BlockSpec((tm, n), lambda i: (i, 0))
    return pl.pallas_call(
        functools.partial(_merge_kernel, alpha=alpha),
        grid=(t_tok // tm,),
        in_specs=[tok(d), mod_spec, mod_spec, mod_spec, tok(ch), tok(ch), tok(ch), tok(ch),
                  pl.BlockSpec((tm, ch), lambda i: (i, s5_col_block)),
                  _const_spec((d, 3 * d)), _const_spec((ch, d)), _const_spec((ch, d)),
                  _const_spec((1, ch)), _const_spec((ch, ch)), _const_spec((1, ch)),
                  _const_spec((ch, d)), _const_spec((d, d)), _const_spec((1, d)), _const_spec((1, d))],
        out_specs=tok(d),
        out_shape=jax.ShapeDtypeStruct((t_tok, d), F32),
        compiler_params=_cparams("arbitrary"),
        name="merge_branches",
    )(x2, shift, scale, gate, z_hy, z_cf, ys[0], ys[1], p, w_gate, w_hy, w_cf, d_skip, w_glu, b_glu,
      w_s5, w_o, ln_g, ln_b)


def _ffn_kernel(x_ref, sh_ref, sc_ref, gt_ref, w1_ref, w3_ref, w2_ref, lg_ref, lb_ref, o_ref, acc_ref,
                *, alpha, tk):
    x = x_ref[...]
    h = _ln_mod(x, sh_ref[0], sc_ref[0]).astype(BF16)
    for k in range(w1_ref.shape[1] // tk):
        cols = slice(k * tk, (k + 1) * tk)
        act = _silu(_bdot(h, w1_ref[:, cols])) * _bdot(h, w3_ref[:, cols])
        y = _bdot(act.astype(BF16), w2_ref[cols, :])
        if k == 0:
            acc_ref[...] = y
        else:
            acc_ref[...] += y
    o_ref[...] = _ln(alpha * x + gt_ref[0] * acc_ref[...]) * lg_ref[...] + lb_ref[...]


def ffn_dense(x2, shift, scale, gate, w1, w3, w2, ln_g, ln_b, *, seq, alpha, shared_mod=False, tk=256):
    t_tok, d = x2.shape
    dff = w1.shape[1]
    tm, brow = _token_tiling(t_tok, seq, 1024, shared_mod)
    mod_spec = pl.BlockSpec((1, 1, d), lambda i: (brow(i), 0, 0))
    return pl.pallas_call(
        functools.partial(_ffn_kernel, alpha=alpha, tk=tk),
        grid=(t_tok // tm,),
        in_specs=[pl.BlockSpec((tm, d), lambda i: (i, 0)), mod_spec, mod_spec, mod_spec,
                  _const_spec((d, dff)), _const_spec((d, dff)), _const_spec((dff, d)),
                  _const_spec((1, d)), _const_spec((1, d))],
        out_specs=pl.BlockSpec((tm, d), lambda i: (i, 0)),
        out_shape=jax.ShapeDtypeStruct((t_tok, d), F32),
        scratch_shapes=[pltpu.VMEM((tm, d), F32)],
        compiler_params=_cparams("arbitrary"),
        name="ffn_dense",
    )(x2, shift, scale, gate, w1, w3, w2, ln_g, ln_b)


def _router_top2(hf, router):
    logits = _dot3(_split_bf16(hf), _split_bf16(router))
    lane = lax.broadcasted_iota(jnp.int32, logits.shape, 1).astype(F32)
    neg = jnp.float32(-jnp.inf)
    logits = jnp.where(lane < N_EXPERTS, logits, neg)
    m1 = jnp.max(logits, axis=-1, keepdims=True)
    i1 = jnp.min(jnp.where(logits == m1, lane, float(ROUTER_LANES)), axis=-1, keepdims=True)
    rest = jnp.where(lane == i1, neg, logits)
    m2 = jnp.max(rest, axis=-1, keepdims=True)
    i2 = jnp.min(jnp.where(rest == m2, lane, float(ROUTER_LANES)), axis=-1, keepdims=True)
    e2 = jnp.exp(m2 - m1)
    return i1, i2, 1.0 / (1.0 + e2), e2 / (1.0 + e2)


def _route_kernel(x_ref, sh_ref, sc_ref, r_ref, o_ref):
    hf = _ln_mod(x_ref[...], sh_ref[0], sc_ref[0])
    i1, i2, w1, w2 = _router_top2(hf, r_ref[...])
    lane = lax.broadcasted_iota(jnp.int32, o_ref.shape, 1)
    o_ref[...] = jnp.where(lane == 0, i1, jnp.where(lane == 1, i2, jnp.where(lane == 2, w1,
                           jnp.where(lane == 3, w2, 0.0))))


def moe_route(x2, shift, scale, router, *, seq):
    t_tok, d = x2.shape
    tm = min(seq, 1024)
    nt = seq // tm
    mod_spec = pl.BlockSpec((1, 1, d), lambda i: (i // nt, 0, 0))
    return pl.pallas_call(
        _route_kernel,
        grid=(t_tok // tm,),
        in_specs=[pl.BlockSpec((tm, d), lambda i: (i, 0)), mod_spec, mod_spec,
                  pl.BlockSpec((d, ROUTER_LANES), lambda i: (0, 0))],
        out_specs=pl.BlockSpec((tm, ROUTE_COLS), lambda i: (i, 0)),
        out_shape=jax.ShapeDtypeStruct((t_tok, ROUTE_COLS), F32),
        compiler_params=_cparams("arbitrary"),
        name="moe_route",
    )(x2, shift, scale, router)


def _dispatch_kernel(pos_ref, x_ref, sh_ref, sc_ref, xs_in_ref, xs_ref, hbuf, sem, *, tm):
    del xs_in_ref
    i = pl.program_id(0)
    n = pl.num_programs(0)
    slot = i % 2

    def row_copy(s, r, p):
        return pltpu.make_async_copy(hbuf.at[s, pl.ds(r, 1)], xs_ref.at[pl.ds(p, 1)], sem.at[s])

    def wait_slot(s):
        for _ in range(TOP_K):
            pltpu.make_async_copy(hbuf.at[s], xs_ref.at[pl.ds(0, tm)], sem.at[s]).wait()

    @pl.when(i >= 2)
    def _():
        wait_slot(slot)

    hbuf[slot] = _ln_mod(x_ref[...], sh_ref[0], sc_ref[0])

    def body(r, carry):
        base = (i * tm + r) * TOP_K
        for k in range(TOP_K):
            row_copy(slot, r, pos_ref[base + k]).start()
        return carry

    lax.fori_loop(0, tm, body, 0, unroll=8)

    @pl.when(i == n - 1)
    def _():
        wait_slot(slot)

    @pl.when((i == n - 1) & (i >= 1))
    def _():
        wait_slot(1 - slot)


def moe_dispatch(pos, x2, shift, scale, n_rows, *, seq, tm=512):
    t_tok, d = x2.shape
    tm = min(tm, seq)
    nt = seq // tm
    mod_spec = pl.BlockSpec((1, 1, d), lambda i, p: (i // nt, 0, 0))
    return pl.pallas_call(
        functools.partial(_dispatch_kernel, tm=tm),
        grid_spec=pltpu.PrefetchScalarGridSpec(
            num_scalar_prefetch=1,
            grid=(t_tok // tm,),
            in_specs=[pl.BlockSpec((tm, d), lambda i, p: (i, 0)), mod_spec, mod_spec,
                      pl.BlockSpec(memory_space=pl.ANY)],
            out_specs=pl.BlockSpec(memory_space=pl.ANY),
            scratch_shapes=[pltpu.VMEM((2, tm, d), F32), pltpu.SemaphoreType.DMA((2,))]),
        out_shape=jax.ShapeDtypeStruct((n_rows, d), F32),
        input_output_aliases={4: 0},
        compiler_params=_cparams("arbitrary"),
        name="moe_dispatch",
    )(pos, x2, shift, scale, jnp.zeros((n_rows, d), F32))


def _expert_kernel(te_ref, na_ref, xs_ref, w1_ref, w3_ref, w2_ref, o_ref, *, tk):
    active = pl.program_id(0) < na_ref[0]

    @pl.when(active)
    def _():
        xb = xs_ref[...].astype(BF16)
        for k in range(w1_ref.shape[-1] // tk):
            cols = slice(k * tk, (k + 1) * tk)
            act = _silu(_bdot(xb, w1_ref[0, :, cols])) * _bdot(xb, w3_ref[0, :, cols])
            y = _bdot(act.astype(BF16), w2_ref[0, cols, :])
            if k == 0:
                o_ref[...] = y
            else:
                o_ref[...] += y

    @pl.when(jnp.logical_not(active))
    def _():
        o_ref[...] = jnp.zeros_like(o_ref)


def moe_experts(tile_expert, n_active, xs, w1, w3, w2, *, tg, tk=512):
    n_rows, d = xs.shape
    dff = w1.shape[-1]
    wspec = lambda shape: pl.BlockSpec(shape, lambda t, te, na: (te[t], 0, 0), pipeline_mode=pl.Buffered(1))
    return pl.pallas_call(
        functools.partial(_expert_kernel, tk=tk),
        grid_spec=pltpu.PrefetchScalarGridSpec(
            num_scalar_prefetch=2,
            grid=(n_rows // tg,),
            in_specs=[pl.BlockSpec((tg, d), lambda t, te, na: (t, 0)),
                      wspec((1, d, dff)), wspec((1, d, dff)), wspec((1, dff, d))],
            out_specs=pl.BlockSpec((tg, d), lambda t, te, na: (t, 0))),
        out_shape=jax.ShapeDtypeStruct((n_rows, d), F32),
        compiler_params=_cparams("arbitrary"),
        name="moe_experts",
    )(tile_expert, n_active, xs, w1, w3, w2)


def _combine_kernel(pos_ref, x_ref, gt_ref, rt_ref, lg_ref, lb_ref, ys_ref, o_ref, ybuf, sem, *, tm, alpha):
    i = pl.program_id(0)
    n = pl.num_programs(0)
    slot = i % 2

    def issue(step, s):
        def body(r, carry):
            base = (step * tm + r) * TOP_K
            for k in range(TOP_K):
                pltpu.make_async_copy(ys_ref.at[pl.ds(pos_ref[base + k], 1)], ybuf.at[s, k, pl.ds(r, 1)],
                                      sem.at[s]).start()
            return carry
        lax.fori_loop(0, tm, body, 0, unroll=8)

    @pl.when(i == 0)
    def _():
        issue(0, 0)

    @pl.when(i + 1 < n)
    def _():
        issue(i + 1, 1 - slot)

    for k in range(TOP_K):
        pltpu.make_async_copy(ys_ref.at[pl.ds(0, tm)], ybuf.at[slot, k], sem.at[slot]).wait()
    rt = rt_ref[...]
    y = rt[:, 2:3] * ybuf[slot, 0] + rt[:, 3:4] * ybuf[slot, 1]
    o_ref[...] = _ln(alpha * x_ref[...] + gt_ref[0] * y) * lg_ref[...] + lb_ref[...]


def moe_combine(pos, x2, gate, route, ln_g, ln_b, ys, *, seq, alpha, tm=512):
    t_tok, d = x2.shape
    tm = min(tm, seq)
    nt = seq // tm
    vec = pl.BlockSpec((1, d), lambda i, p: (0, 0))
    return pl.pallas_call(
        functools.partial(_combine_kernel, tm=tm, alpha=alpha),
        grid_spec=pltpu.PrefetchScalarGridSpec(
            num_scalar_prefetch=1,
            grid=(t_tok // tm,),
            in_specs=[pl.BlockSpec((tm, d), lambda i, p: (i, 0)),
                      pl.BlockSpec((1, 1, d), lambda i, p: (i // nt, 0, 0)),
                      pl.BlockSpec((tm, ROUTE_COLS), lambda i, p: (i, 0)), vec, vec,
                      pl.BlockSpec(memory_space=pl.ANY)],
            out_specs=pl.BlockSpec((tm, d), lambda i, p: (i, 0)),
            scratch_shapes=[pltpu.VMEM((2, TOP_K, tm, d), F32), pltpu.SemaphoreType.DMA((2,))]),
        out_shape=jax.ShapeDtypeStruct((t_tok, d), F32),
        compiler_params=_cparams("arbitrary"),
        name="moe_combine",
    )(pos, x2, gate, route, ln_g, ln_b, ys)


def _dispatch_plan(route, *, tg):
    ids = route[:, 0:TOP_K].astype(jnp.int32).reshape(-1)
    n_assign = ids.shape[0]
    onehot = (ids[:, None] == jnp.arange(N_EXPERTS, dtype=jnp.int32)[None, :]).astype(jnp.int32)
    csum = jnp.cumsum(onehot, axis=0)
    counts = csum[-1]
    padded = (counts + tg - 1) // tg * tg
    ends = jnp.cumsum(padded)
    starts = ends - padded
    pos = jnp.sum(onehot * (starts[None, :] + csum - 1), axis=1).astype(jnp.int32)
    n_tiles = n_assign // tg + N_EXPERTS
    tile_start = jnp.arange(n_tiles, dtype=jnp.int32) * tg
    tile_expert = jnp.minimum(jnp.sum(tile_start[:, None] >= ends[None, :], axis=1), N_EXPERTS - 1)
    n_active = (ends[-1] // tg).reshape(1)
    return pos, tile_expert.astype(jnp.int32), n_active.astype(jnp.int32), n_tiles * tg


def ffn_moe(x2, shift, scale, gate, router, w1, w3, w2, ln_g, ln_b, *, seq, alpha, tg=1024):
    route = moe_route(x2, shift, scale, router, seq=seq)
    pos, tile_expert, n_active, n_rows = _dispatch_plan(route, tg=tg)
    xs = moe_dispatch(pos, x2, shift, scale, n_rows, seq=seq)
    ys = moe_experts(tile_expert, n_active, xs, w1, w3, w2, tg=tg)
    return moe_combine(pos, x2, gate, route, ln_g, ln_b, ys, seq=seq, alpha=alpha)


def _dft_matrices(rows, cols, n):
    f = lax.broadcasted_iota(jnp.int32, (rows, cols), 0)
    t = lax.broadcasted_iota(jnp.int32, (rows, cols), 1)
    ang = ((f * t) % n).astype(F32) * (2.0 * math.pi / n)
    return jnp.cos(ang), jnp.sin(ang)


def _twiddles(seq, width):
    ang = jnp.arange(seq // 2, dtype=F32) * (math.pi / seq)
    return jnp.broadcast_to(jnp.stack([jnp.cos(ang), jnp.sin(ang)])[:, :, None], (2, seq // 2, width))


def _position_features(seq, n_bands, width):
    t = jnp.arange(seq, dtype=F32)[:, None]
    tn = t / max(seq - 1, 1)
    bands = jnp.arange(1, n_bands + 1, dtype=F32)
    ang = t * bands * (2.0 * math.pi / seq)
    z = jnp.concatenate([tn, jnp.cos(ang), jnp.sin(ang)], axis=-1)
    return jnp.pad(z, ((0, 0), (0, width - z.shape[1])))


def _s5_discretize(a_re, a_im, log_dt, b_re, b_im):
    lam_re = jnp.minimum(a_re, S5_MAX_RE)
    lam_im = a_im
    dt = jnp.exp(log_dt)[..., None]
    mag = jnp.exp(lam_re * dt)
    ang = lam_im * dt
    abar_re = mag * jnp.cos(ang)
    abar_im = mag * jnp.sin(ang)
    den = lam_re * lam_re + lam_im * lam_im
    q_re = ((abar_re - 1.0) * lam_re + abar_im * lam_im) / den
    q_im = (abar_im * lam_re - (abar_re - 1.0) * lam_im) / den
    bb_re = q_re[..., None] * b_re - q_im[..., None] * b_im
    bb_im = q_re[..., None] * b_im + q_im[..., None] * b_re
    return abar_re, abar_im, bb_re, bb_im


def _s5_matrices(a_re, a_im, log_dt, b_re, b_im, c_re, c_im):
    abr, abi, bbr, bbi = _s5_discretize(a_re, a_im, log_dt, b_re, b_im)
    g_all, p, k = bbr.shape[1:]
    gb = S5_BLOCK_CH // S5_GROUP
    nj = g_all // gb
    eye = jnp.eye(gb, dtype=F32)

    def in_mat(bb):
        t = jnp.swapaxes(bb, -1, -2).reshape(2, nj, gb, k, p)
        return jnp.einsum("djgkp,gh->djgkhp", t, eye).reshape(2, nj, gb * k, gb * p)

    def out_mat(cc):
        t = jnp.swapaxes(cc, -1, -2).reshape(2, nj, gb, p, k)
        return jnp.einsum("djhpc,hg->djhpgc", t, eye).reshape(2, nj, gb * p, gb * k)

    bmat = jnp.concatenate([in_mat(bbr), in_mat(bbi)], axis=-1).astype(BF16)
    cmat = jnp.concatenate([out_mat(c_re), -out_mat(c_im)], axis=-2).astype(BF16)
    abar = jnp.stack([abr.reshape(2, nj, gb * p), abi.reshape(2, nj, gb * p)], axis=2)
    return bmat, cmat, abar


def kernel(x, c, ctx, c_ctx, w_mod, b_mod, w_in, hy_short_w, hy_short_b, hy_f_w1, hy_f_b1, hy_f_w2, hy_f_b2, hy_f_w3, hy_f_b3, hy_freq, hy_decay, hy_bias, w_hy_out, cf_dw_w, cf_dw_b, cf_ln_g, cf_ln_b, w_cf_out, s5_a_re, s5_a_im, s5_log_dt, s5_b_re, s5_b_im, s5_c_re, s5_c_im, s5_d, s5_w_glu, s5_b_glu, w_s5_out, w_o, ln1_g, ln1_b, ln2_g, ln2_b, ffn_w1, ffn_w3, ffn_w2, moe_router, moe_w1, moe_w3, moe_w2):
    bsz, seq, d = x.shape
    seq_c = ctx.shape[1]
    depth = w_mod.shape[0]
    alpha = (2.0 * depth) ** 0.25
    ch = D_BRANCH
    hy_cols = (HY_ORDER + 1) * ch
    cf_off = hy_cols
    s5_off = cf_off + 2 * ch
    gate_off = s5_off + ch
    n_bands = (hy_f_w1.shape[1] - 1) // 2
    nj = ch // S5_BLOCK_CH

    x2 = x.reshape(bsz * seq, d)
    xc2 = ctx.reshape(bsz * seq_c, d)
    mod_rows = 24
    cc = jnp.zeros((mod_rows, d), F32).at[:bsz].set(c).at[bsz].set(c_ctx)

    dft = {}
    feats = {}
    for s in (seq, seq_c):
        cs, ss = _dft_matrices(s // 2, s, 2 * s)
        csh, ssh = _dft_matrices(s // 2, s // 2, s)
        dft[s] = (cs, ss, csh.astype(BF16), ssh.astype(BF16), _twiddles(s, ch))
        feats[s] = _position_features(s, n_bands, 128)
    zero_state = jnp.zeros((2, nj, bsz, 2 * S5_BLOCK_ST), F32)
    row = lambda v: v[None, :]

    for l in range(depth):
        last = l == depth - 1
        mods = mod_vectors(cc, w_mod[l].astype(BF16), row(b_mod[l])).reshape(mod_rows, 6, d)
        lat = [mods[:bsz, i][:, None, :] for i in range(6)]
        cxm = [jnp.broadcast_to(mods[bsz, i][None, None, :], (bsz, 1, d)) for i in range(6)]

        w_in_b = w_in[l].astype(BF16)
        w_mix = w_in_b[:, :gate_off]
        w_s5 = w_in_b[:, s5_off:gate_off]
        w_gate = w_in_b[:, gate_off:]
        sw = jnp.concatenate([hy_short_w[l][:, i * ch:(i + 1) * ch] for i in range(HY_ORDER + 1)], axis=0)
        sb = hy_short_b[l].reshape(HY_ORDER + 1, ch)
        w1p = jnp.pad(hy_f_w1[l], ((0, 128 - hy_f_w1.shape[1]), (0, 0)))
        bmat, cmat, abar = _s5_matrices(s5_a_re[l], s5_a_im[l], s5_log_dt[l], s5_b_re[l], s5_b_im[l],
                                        s5_c_re[l], s5_c_im[l])
        merge_w = (w_gate, w_hy_out[l].astype(BF16), w_cf_out[l].astype(BF16), row(s5_d[l]),
                   s5_w_glu[l].astype(BF16), row(s5_b_glu[l]), w_s5_out[l].astype(BF16),
                   w_o[l].astype(BF16), row(ln1_g[l]), row(ln1_b[l]))

        def spectra(s):
            filt = hyena_filters(feats[s], w1p, row(hy_f_b1[l]), hy_f_w2[l], row(hy_f_b2[l]), hy_f_w3[l],
                                 row(hy_f_b3[l]), hy_freq[l], row(hy_decay[l]))
            return hyena_spectra(dft[s][0], dft[s][1], filt)

        def token_mixer(xs, s, seg, m, init, shared):
            sh1, sc1, g1 = m[0], m[1], m[2]
            p = ln_mod_proj(xs, sh1, sc1, w_mix, seq=s, shared_mod=shared)
            z_hy = hyena_mix(p, sw, sb, dft[s][2], dft[s][3], dft[s][4], *spectra(s), hy_bias[l], seq=s)
            z_cf = conformer_mix(p, cf_dw_w[l], row(cf_dw_b[l]), row(cf_ln_g[l]), row(cf_ln_b[l]), seg=seg,
                                 col_block=cf_off // ch)
            ys, fin = s5_scan(p, bmat, cmat, abar, init, seq=s, col_off=s5_off)
            x1 = merge_branches(xs, sh1, sc1, g1, z_hy, z_cf, ys, p, *merge_w, seq=s, alpha=alpha,
                                s5_col_block=s5_off // ch, shared_mod=shared)
            return x1, fin

        def channel_mixer(xs, s, m, shared):
            sh2, sc2, g2 = m[3], m[4], m[5]
            i = l // 2
            if l % 2 == 0:
                return ffn_dense(xs, sh2, sc2, g2, ffn_w1[i].astype(BF16), ffn_w3[i].astype(BF16),
                                 ffn_w2[i].astype(BF16), row(ln2_g[l]), row(ln2_b[l]), seq=s, alpha=alpha,
                                 shared_mod=shared)
            router = jnp.pad(moe_router[i], ((0, 0), (0, ROUTER_LANES - moe_router.shape[-1])))
            return ffn_moe(xs, sh2, sc2, g2, router, moe_w1[i].astype(BF16), moe_w3[i].astype(BF16),
                           moe_w2[i].astype(BF16), row(ln2_g[l]), row(ln2_b[l]), seq=s, alpha=alpha)

        if last:
            u_s5 = ln_mod_proj(xc2, cxm[0], cxm[1], w_s5, seq=seq_c, shared_mod=True)
            _, finals = s5_scan(u_s5, bmat, cmat, abar, zero_state, seq=seq_c, col_off=0)
        else:
            xc2, finals = token_mixer(xc2, seq_c, seq_c, cxm, zero_state, True)
            xc2 = channel_mixer(xc2, seq_c, cxm, True)

        x2, _ = token_mixer(x2, seq, GRID_W, lat, finals, False)
        x2 = channel_mixer(x2, seq, lat, False)
    return x2.reshape(bsz, seq, d)
```

```python
import functools
import math

import jax
import jax.numpy as jnp
from jax import lax
from jax.experimental import pallas as pl
from jax.experimental.pallas import tpu as pltpu

F32 = jnp.float32
BF16 = jnp.bfloat16
HIGHEST = lax.Precision.HIGHEST

V7X_VMEM_LIMIT_BYTES = 56 * 1024 * 1024

LN_EPS = 1e-5
D_BRANCH = 512
HY_ORDER = 2
HY_SHORT = 3
HY_ROW_BLOCK = 512
CF_KERNEL = 31
SUBLANES = 8
LANES = 128
CF_PAD = 16
CF_SHIFT_SPAN = 24
GRID_W = 64
S5_GROUP = 16
S5_STATE = 64
S5_MAX_RE = -1e-4
S5_TIME_BLOCK = 16
S5_SUB_CHUNK = 32
S5_BLOCK_CH = 128
S5_BLOCK_ST = S5_BLOCK_CH // S5_GROUP * S5_STATE
N_EXPERTS = 8
TOP_K = 2
ROUTER_LANES = 128
ROUTE_COLS = 8


def _cparams(*sem):
    return pltpu.CompilerParams(dimension_semantics=sem, vmem_limit_bytes=V7X_VMEM_LIMIT_BYTES)


def _const_spec(shape):
    nd = len(shape)
    return pl.BlockSpec(shape, lambda *_: (0,) * nd, pipeline_mode=pl.Buffered(1))


def _bdot(a, b):
    return jnp.dot(a, b, preferred_element_type=F32)


def _split_bf16(a):
    hi = a.astype(BF16)
    return hi, (a - hi.astype(F32)).astype(BF16)


def _dot3(a, b):
    return _bdot(a[0], b[0]) + (_bdot(a[1], b[0]) + _bdot(a[0], b[1]))


def _sigmoid(x):
    return 1.0 / (1.0 + jnp.exp(-x))


def _silu(x):
    return x * _sigmoid(x)


def _gelu_tanh(x):
    return 0.5 * x * (1.0 + jnp.tanh(math.sqrt(2.0 / math.pi) * (x + 0.044715 * (x * x * x))))


def _ln(x):
    mu = jnp.mean(x, axis=-1, keepdims=True)
    xc = x - mu
    var = jnp.mean(xc * xc, axis=-1, keepdims=True)
    return xc * lax.rsqrt(var + LN_EPS)


def _ln_mod(x, shift, scale):
    return _ln(x) * (1.0 + scale) + shift


def _token_tiling(t_tok, seq, cap, shared_mod):
    if shared_mod:
        return min(t_tok, cap), lambda i: 0
    tm = min(seq, cap)
    nt = seq // tm
    return tm, lambda i: i // nt


def _mod_kernel(c_ref, w_ref, b_ref, o_ref):
    s = _silu(c_ref[...])
    o_ref[...] = _bdot(s.astype(BF16), w_ref[...]) + b_ref[...]


def mod_vectors(cc, w, b, *, tn=1536):
    r, d = cc.shape
    n = w.shape[1]
    return pl.pallas_call(
        _mod_kernel,
        grid=(n // tn,),
        in_specs=[pl.BlockSpec((r, d), lambda j: (0, 0)),
                  pl.BlockSpec((d, tn), lambda j: (0, j)),
                  pl.BlockSpec((1, tn), lambda j: (0, j))],
        out_specs=pl.BlockSpec((r, tn), lambda j: (0, j)),
        out_shape=jax.ShapeDtypeStruct((r, n), F32),
        compiler_params=_cparams("arbitrary"),
        name="mod_vectors",
    )(cc, w, b)


def _proj_kernel(x_ref, sh_ref, sc_ref, w_ref, o_ref, *, tn):
    h = _ln_mod(x_ref[...], sh_ref[0], sc_ref[0]).astype(BF16)
    for j in range(o_ref.shape[1] // tn):
        cols = slice(j * tn, (j + 1) * tn)
        o_ref[:, cols] = _bdot(h, w_ref[:, cols])


def ln_mod_proj(x2, shift, scale, w, *, seq, shared_mod=False, tn=512):
    t_tok, d = x2.shape
    n = w.shape[1]
    tm, brow = _token_tiling(t_tok, seq, 1024, shared_mod)
    mod_spec = pl.BlockSpec((1, 1, d), lambda i: (brow(i), 0, 0))
    return pl.pallas_call(
        functools.partial(_proj_kernel, tn=tn),
        grid=(t_tok // tm,),
        in_specs=[pl.BlockSpec((tm, d), lambda i: (i, 0)), mod_spec, mod_spec, _const_spec((d, n))],
        out_specs=pl.BlockSpec((tm, n), lambda i: (i, 0)),
        out_shape=jax.ShapeDtypeStruct((t_tok, n), F32),
        compiler_params=_cparams("arbitrary"),
        name="ln_mod_proj",
    )(x2, shift, scale, w)


def _filter_kernel(z_ref, w1_ref, b1_ref, w2_ref, b2_ref, w3_ref, b3_ref, fr_ref, dec_ref, o_ref):
    z = z_ref[...]
    h = jnp.sin(fr_ref[0:1, :] * (jnp.dot(z, w1_ref[...], precision=HIGHEST, preferred_element_type=F32)
                                  + b1_ref[...]))
    h = jnp.sin(fr_ref[1:2, :] * (jnp.dot(h, w2_ref[...], precision=HIGHEST, preferred_element_type=F32)
                                  + b2_ref[...]))
    h = jnp.dot(h, w3_ref[...], precision=HIGHEST, preferred_element_type=F32) + b3_ref[...]
    tn = z[:, 0:1]
    o_ref[...] = h * jnp.exp(-tn * jnp.abs(dec_ref[...]))


def hyena_filters(z, w1, b1, w2, b2, w3, b3, freq, decay, *, tn=512):
    seq, zf = z.shape
    hid = w2.shape[0]
    n = w3.shape[1]
    full = lambda shape: pl.BlockSpec(shape, lambda j: (0, 0))
    return pl.pallas_call(
        _filter_kernel,
        grid=(n // tn,),
        in_specs=[full((seq, zf)), full((zf, hid)), full((1, hid)), full((hid, hid)), full((1, hid)),
                  pl.BlockSpec((hid, tn), lambda j: (0, j)), pl.BlockSpec((1, tn), lambda j: (0, j)),
                  full((2, hid)), pl.BlockSpec((1, tn), lambda j: (0, j))],
        out_specs=pl.BlockSpec((seq, tn), lambda j: (0, j)),
        out_shape=jax.ShapeDtypeStruct((seq, n), F32),
        compiler_params=_cparams("arbitrary"),
        name="hyena_filters",
    )(z, w1, b1, w2, b2, w3, b3, freq, decay)


def _spectrum_kernel(cs_ref, ss_ref, filt_ref, k1r_ref, k1i_ref, k2r_ref, k2i_ref, km_ref, *, tf, n_fft):
    ch = k1r_ref.shape[-1]
    seq = filt_ref.shape[0]
    hf = filt_ref[:, 0:ch]
    tau = lax.broadcasted_iota(jnp.int32, (seq, 1), 0)
    hb = jnp.where(tau == 0, 0.0, filt_ref[:, ch:2 * ch])
    s_cos = hf + hb
    s_sin = hb - hf
    sign = (1 - 2 * (tau & 1)).astype(F32)
    f = pl.program_id(1) * tf + lax.broadcasted_iota(jnp.int32, (tf, 1), 0)
    wgt = jnp.where(f == 0, 1.0, 2.0) / n_fft
    cs = _split_bf16(cs_ref[...])
    ss = _split_bf16(ss_ref[...])
    k1r_ref[0] = wgt * _dot3(cs, _split_bf16(s_cos))
    k1i_ref[0] = wgt * _dot3(ss, _split_bf16(s_sin))
    k2r_ref[0] = wgt * _dot3(cs, _split_bf16(sign * s_cos))
    k2i_ref[0] = -wgt * _dot3(ss, _split_bf16(sign * s_sin))
    quarter = tau & 3
    c4 = jnp.where(quarter == 0, 1.0, jnp.where(quarter == 2, -1.0, 0.0))
    s4 = jnp.where(quarter == 1, 1.0, jnp.where(quarter == 3, -1.0, 0.0))
    km_ref[0, 0:1, :] = jnp.sum(c4 * s_cos, axis=0, keepdims=True) * (2.0 / n_fft)
    km_ref[0, 1:2, :] = jnp.sum(s4 * s_sin, axis=0, keepdims=True) * (2.0 / n_fft)


def hyena_spectra(cs, ss, filt, *, tf=256):
    m, seq = cs.shape
    ch = D_BRANCH
    tf = min(tf, m)
    rows = pl.BlockSpec((tf, seq), lambda o, f: (f, 0))
    out = pl.BlockSpec((1, tf, ch), lambda o, f: (o, f, 0))
    return pl.pallas_call(
        functools.partial(_spectrum_kernel, tf=tf, n_fft=2 * seq),
        grid=(HY_ORDER, m // tf),
        in_specs=[rows, rows, pl.BlockSpec((seq, 2 * ch), lambda o, f: (0, o))],
        out_specs=[out] * 4 + [pl.BlockSpec((1, 2, ch), lambda o, f: (o, 0, 0))],
        out_shape=[jax.ShapeDtypeStruct((HY_ORDER, m, ch), F32)] * 4
                  + [jax.ShapeDtypeStruct((HY_ORDER, 2, ch), F32)],
        compiler_params=_cparams("arbitrary", "arbitrary"),
        name="hyena_spectra",
    )(cs, ss, filt)


def _hyena_kernel(*refs, nh):
    parts = [refs[p * nh:(p + 1) * nh] for p in range(HY_ORDER + 1)]
    (sw_ref, sb_ref, cs_ref, ss_ref, tw_ref, k1r_ref, k1i_ref, k2r_ref, k2i_ref, km_ref, bias_ref,
     o_ref, ze_ref, zo_ref, xe_ref, xo_ref, zeb_ref, zob_ref, g_ref, il_ref) = refs[(HY_ORDER + 1) * nh:]
    m = o_ref.shape[0] // 2
    row = lax.broadcasted_iota(jnp.int32, (m, 1), 0)
    sign = (1 - 2 * (row & 1)).astype(F32)

    def short_conv(part):
        pe = jnp.concatenate([r[pl.ds(0, m, stride=2), :] for r in parts[part]], axis=1)
        po = jnp.concatenate([r[pl.ds(1, m, stride=2), :] for r in parts[part]], axis=1)
        po_prev = jnp.where(row == 0, 0.0, pltpu.roll(po, 1, 0))
        pe_next = jnp.where(row == m - 1, 0.0, pltpu.roll(pe, m - 1, 0))
        w = sw_ref[part * HY_SHORT:(part + 1) * HY_SHORT, :]
        b = sb_ref[part:part + 1, :]
        return (w[0:1] * po_prev + w[1:2] * pe + w[2:3] * po + b,
                w[0:1] * pe + w[1:2] * po + w[2:3] * pe_next + b)

    rb = min(m, HY_ROW_BLOCK)
    blocks = [slice(r * rb, (r + 1) * rb) for r in range(m // rb)]
    ze_ref[...], zo_ref[...] = short_conv(HY_ORDER)
    for o in range(HY_ORDER):
        ze = ze_ref[...]
        zo = zo_ref[...]
        zeb_ref[...] = ze.astype(BF16)
        zob_ref[...] = zo.astype(BF16)
        e_m = jnp.sum(sign * ze, axis=0, keepdims=True)
        o_m = jnp.sum(sign * zo, axis=0, keepdims=True)
        kmr = km_ref[o, 0:1, :]
        kmi = km_ref[o, 1:2, :]
        nyq_e = e_m * kmr + o_m * kmi
        nyq_o = o_m * kmr - e_m * kmi
        bias = bias_ref[o:o + 1, :]
        xe_ref[...], xo_ref[...] = short_conv(o)
        for rows in blocks:
            cs = cs_ref[rows, :]
            ss = ss_ref[rows, :]
            ae = _bdot(cs, zeb_ref[...])
            be = _bdot(ss, zeb_ref[...])
            ao = _bdot(cs, zob_ref[...])
            bo = _bdot(ss, zob_ref[...])
            c = tw_ref[0, rows, :]
            s = tw_ref[1, rows, :]
            tr = c * ao - s * bo
            ti = -(c * bo + s * ao)
            pr, pi = ae + tr, ti - be
            qr, qi = ae - tr, -be - ti
            k1r, k1i = k1r_ref[o, rows, :], k1i_ref[o, rows, :]
            k2r, k2i = k2r_ref[o, rows, :], k2i_ref[o, rows, :]
            z1r = pr * k1r - pi * k1i
            z1i = pr * k1i + pi * k1r
            z2r = qr * k2r + qi * k2i
            z2i = qr * k2i - qi * k2r
            dr, di = z1r - z2r, z1i + z2i
            g_ref[0, rows, :] = (z1r + z2r).astype(BF16)
            g_ref[1, rows, :] = (z1i - z2i).astype(BF16)
            g_ref[2, rows, :] = (c * dr - s * di).astype(BF16)
            g_ref[3, rows, :] = (c * di + s * dr).astype(BF16)
        for rows in blocks:
            cs = cs_ref[rows, :]
            ss = ss_ref[rows, :]
            ye = _bdot(cs, g_ref[0]) - _bdot(ss, g_ref[1]) + sign[rows] * nyq_e + bias * ze_ref[rows, :]
            yo = _bdot(cs, g_ref[2]) - _bdot(ss, g_ref[3]) + sign[rows] * nyq_o + bias * zo_ref[rows, :]
            ze_ref[rows, :] = xe_ref[rows, :] * ye
            zo_ref[rows, :] = xo_ref[rows, :] * yo
    for h in range(nh):
        lanes = slice(h * LANES, (h + 1) * LANES)
        il_ref[pl.ds(0, m, stride=2), :] = ze_ref[:, lanes]
        il_ref[pl.ds(1, m, stride=2), :] = zo_ref[:, lanes]
        o_ref[:, lanes] = il_ref[...]


def hyena_mix(p, sw, sb, cs, ss, tw, k1r, k1i, k2r, k2i, km, bias, *, seq, cb=256):
    t_tok = p.shape[0]
    ch = D_BRANCH
    bsz = t_tok // seq
    m = seq // 2
    nh = cb // LANES
    col_blocks = ch // LANES
    cols = [pl.BlockSpec((seq, LANES), functools.partial(lambda j, b, c0: (b, c0 + j * nh), c0=part * col_blocks + h))
            for part in range(HY_ORDER + 1) for h in range(nh)]
    per_c = lambda rows: pl.BlockSpec((rows, cb), lambda j, b: (0, j))
    spec3 = lambda lead, rows: pl.BlockSpec((lead, rows, cb), lambda j, b: (0, 0, j),
                                            pipeline_mode=pl.Buffered(1))
    return pl.pallas_call(
        functools.partial(_hyena_kernel, nh=nh),
        grid=(ch // cb, bsz),
        in_specs=cols + [per_c(3 * HY_SHORT), per_c(3), _const_spec((m, m)), _const_spec((m, m)),
                         spec3(2, m)] + [spec3(HY_ORDER, m)] * 4 + [spec3(HY_ORDER, 2), per_c(HY_ORDER)],
        out_specs=pl.BlockSpec((seq, cb), lambda j, b: (b, j)),
        out_shape=jax.ShapeDtypeStruct((t_tok, ch), F32),
        scratch_shapes=[pltpu.VMEM((m, cb), F32)] * 4 + [pltpu.VMEM((m, cb), BF16)] * 2
                       + [pltpu.VMEM((4, m, cb), BF16), pltpu.VMEM((seq, LANES), F32)],
        compiler_params=_cparams("arbitrary", "arbitrary"),
        name="hyena_mix",
    )(*([p] * ((HY_ORDER + 1) * nh)), sw, sb, cs, ss, tw, k1r, k1i, k2r, k2i, km, bias)


def _conformer_kernel(a_ref, gl_ref, w_ref, b_ref, g_ref, be_ref, o_ref, pad_ref, shift_ref, *, seg, nseg):
    ch = a_ref.shape[-1]
    zeros = jnp.zeros((nseg, CF_PAD, ch), F32)
    pad_ref[:, 0:CF_PAD, :] = zeros
    pad_ref[:, CF_PAD + seg:, :] = zeros
    u = a_ref[...] * _sigmoid(gl_ref[...])
    pad_ref[:, CF_PAD:CF_PAD + seg, :] = u.reshape(nseg, seg, ch)
    span = seg + CF_SHIFT_SPAN
    for r in range(1, SUBLANES):
        shift_ref[r - 1] = pad_ref[:, r:r + span, :]
    half = (CF_KERNEL - 1) // 2

    def segment(s, carry):
        for c in range(ch // LANES):
            lanes = slice(c * LANES, (c + 1) * LANES)
            acc = jnp.zeros((seg, LANES), F32)
            for k in range(CF_KERNEL):
                q, r = divmod(CF_PAD - half + k, SUBLANES)
                rows = slice(q * SUBLANES, q * SUBLANES + seg)
                tap = pad_ref[s, rows, lanes] if r == 0 else shift_ref[r - 1, s, rows, lanes]
                acc = acc + w_ref[k:k + 1, lanes] * tap
            o_ref[pl.ds(pl.multiple_of(s * seg, seg), seg), lanes] = acc
        return carry

    lax.fori_loop(0, nseg, segment, 0)
    y = _ln(o_ref[...] + b_ref[...]) * g_ref[...] + be_ref[...]
    o_ref[...] = _silu(y)


def conformer_mix(p, dw_w, dw_b, ln_g, ln_b, *, seg, col_block):
    t_tok = p.shape[0]
    ch = D_BRANCH
    tm = max(seg, 512)
    nseg = tm // seg
    vec = lambda rows: pl.BlockSpec((rows, ch), lambda i: (0, 0))
    return pl.pallas_call(
        functools.partial(_conformer_kernel, seg=seg, nseg=nseg),
        grid=(t_tok // tm,),
        in_specs=[pl.BlockSpec((tm, ch), lambda i: (i, col_block)),
                  pl.BlockSpec((tm, ch), lambda i: (i, col_block + 1)),
                  vec(CF_KERNEL), vec(1), vec(1), vec(1)],
        out_specs=pl.BlockSpec((tm, ch), lambda i: (i, 0)),
        out_shape=jax.ShapeDtypeStruct((t_tok, ch), F32),
        scratch_shapes=[pltpu.VMEM((nseg, seg + 2 * CF_PAD, ch), F32),
                        pltpu.VMEM((SUBLANES - 1, nseg, seg + CF_SHIFT_SPAN, ch), F32)],
        compiler_params=_cparams("arbitrary"),
        name="conformer_mix",
    )(p, p, dw_w, dw_b, ln_g, ln_b)


def _s5_kernel(u_ref, b_ref, c_ref, a_ref, init_ref, y_ref, fin_ref, u1_ref, utm_ref, bu_ref, sb_ref,
               ytm_ref, st_ref, *, tc, bsz, reverse):
    c = pl.program_id(1)
    ns = S5_BLOCK_ST

    @pl.when(c == 0)
    def _():
        st_ref[...] = init_ref[0]

    tb = S5_TIME_BLOCK
    for th in range(tc // tb):
        for b in range(bsz):
            u1_ref[pl.ds((th * bsz + b) * tb, tb), :] = u_ref[b, th * tb:(th + 1) * tb, :]
    for th in range(tc // tb):
        for tl in range(tb):
            utm_ref[pl.ds((th * tb + tl) * bsz, bsz), :] = u1_ref[pl.ds(th * bsz * tb + tl, bsz, stride=tb), :]
    a_re = jnp.broadcast_to(a_ref[0, 0:1, :], (bsz, ns))
    a_im = jnp.broadcast_to(a_ref[0, 1:2, :], (bsz, ns))
    s_re = st_ref[:, 0:ns]
    s_im = st_ref[:, ns:2 * ns]
    sub = min(tc, S5_SUB_CHUNK)
    order = (lambda n: range(n - 1, -1, -1)) if reverse else range
    for sc in order(tc // sub):
        r0 = sc * sub * bsz
        rows = slice(r0, r0 + sub * bsz)
        bu_ref[rows, :] = _bdot(utm_ref[rows, :].astype(BF16), b_ref[0])
        for tl in order(sub):
            rr = slice(r0 + tl * bsz, r0 + (tl + 1) * bsz)
            n_re = a_re * s_re - a_im * s_im + bu_ref[rr, 0:ns]
            n_im = a_re * s_im + a_im * s_re + bu_ref[rr, ns:2 * ns]
            sb_ref[rr, 0:ns] = n_re.astype(BF16)
            sb_ref[rr, ns:2 * ns] = n_im.astype(BF16)
            s_re, s_im = n_re, n_im
        ytm_ref[rows, :] = _bdot(sb_ref[rows, :], c_ref[0])
    st_ref[:, 0:ns] = s_re
    st_ref[:, ns:2 * ns] = s_im
    for b in range(bsz):
        y_ref[b] = ytm_ref[pl.ds(b, tc, stride=bsz), :]

    @pl.when(c == pl.num_programs(1) - 1)
    def _():
        fin_ref[0] = st_ref[...]


def s5_scan(p, bmat, cmat, abar, init, *, seq, col_off, tc=128):
    t_tok, n_cols = p.shape
    ch = D_BRANCH
    bsz = t_tok // seq
    u3 = p.reshape(bsz, seq, n_cols)
    j0 = col_off // S5_BLOCK_CH
    nj = ch // S5_BLOCK_CH
    tc = min(tc, seq)
    nc = seq // tc
    rows = tc * bsz
    ns2 = 2 * S5_BLOCK_ST
    par = lambda r, cdim: pl.BlockSpec((1, r, cdim), lambda j, c: (j, 0, 0))
    ys, fins = [], []
    for d, reverse in enumerate((False, True)):
        chunk = (lambda c: nc - 1 - c) if reverse else (lambda c: c)
        y, fin = pl.pallas_call(
            functools.partial(_s5_kernel, tc=tc, bsz=bsz, reverse=reverse),
            grid=(nj, nc),
            in_specs=[pl.BlockSpec((bsz, tc, S5_BLOCK_CH),
                                   functools.partial(lambda j, c, ck: (0, ck(c), j0 + j), ck=chunk)),
                      par(S5_BLOCK_CH, ns2), par(ns2, S5_BLOCK_CH), par(2, S5_BLOCK_ST), par(bsz, ns2)],
            out_specs=[pl.BlockSpec((bsz, tc, S5_BLOCK_CH),
                                    functools.partial(lambda j, c, ck: (0, ck(c), j), ck=chunk)),
                       par(bsz, ns2)],
            out_shape=[jax.ShapeDtypeStruct((bsz, seq, ch), F32),
                       jax.ShapeDtypeStruct((nj, bsz, ns2), F32)],
            scratch_shapes=[pltpu.VMEM((rows, S5_BLOCK_CH), F32), pltpu.VMEM((rows, S5_BLOCK_CH), F32),
                            pltpu.VMEM((rows, ns2), F32), pltpu.VMEM((rows, ns2), BF16),
                            pltpu.VMEM((rows, S5_BLOCK_CH), F32), pltpu.VMEM((bsz, ns2), F32)],
            compiler_params=_cparams("arbitrary", "arbitrary"),
            name="s5_scan_bwd" if reverse else "s5_scan_fwd",
        )(u3, bmat[d], cmat[d], abar[d], init[d])
        ys.append(y.reshape(t_tok, ch))
        fins.append(fin)
    return ys, jnp.stack(fins)


def _merge_kernel(x_ref, sh_ref, sc_ref, gt_ref, zh_ref, zc_ref, ysf_ref, ysb_ref, us_ref,
                  wg_ref, why_ref, wcf_ref, ds_ref, wglu_ref, bglu_ref, ws5_ref, wo_ref,
                  lg_ref, lb_ref, o_ref, *, alpha):
    d = x_ref.shape[-1]
    x = x_ref[...]
    h = _ln_mod(x, sh_ref[0], sc_ref[0]).astype(BF16)
    gate = lambda i: _sigmoid(_bdot(h, wg_ref[:, i * d:(i + 1) * d]))
    mix = gate(0) * _bdot(zh_ref[...].astype(BF16), why_ref[...])
    mix = mix + gate(1) * _bdot(zc_ref[...].astype(BF16), wcf_ref[...])
    ys = ysf_ref[...] + ysb_ref[...] + ds_ref[...] * us_ref[...]
    g = _gelu_tanh(ys)
    y5 = g * _sigmoid(_bdot(g.astype(BF16), wglu_ref[...]) + bglu_ref[...])
    mix = mix + gate(2) * _bdot(y5.astype(BF16), ws5_ref[...])
    y = _bdot(mix.astype(BF16), wo_ref[...])
    o_ref[...] = _ln(alpha * x + gt_ref[0] * y) * lg_ref[...] + lb_ref[...]


def merge_branches(x2, shift, scale, gate, z_hy, z_cf, ys, p, w_gate, w_hy, w_cf, d_skip,
                   w_glu, b_glu, w_s5, w_o, ln_g, ln_b, *, seq, alpha, s5_col_block, shared_mod=False):
    t_tok, d = x2.shape
    ch = D_BRANCH
    tm, brow = _token_tiling(t_tok, seq, 512, shared_mod)
    mod_spec = pl.BlockSpec((1, 1, d), lambda i: (brow(i), 0, 0))
    tok = lambda n: pl.BlockSpec((tm, n), lambda i: (i, 0))
    return pl.pallas_call(
        functools.partial(_merge_kernel, alpha=alpha),
        grid=(t_tok // tm,),
        in_specs=[tok(d), mod_spec, mod_spec, mod_spec, tok(ch), tok(ch), tok(ch), tok(ch),
                  pl.BlockSpec((tm, ch), lambda i: (i, s5_col_block)),
                  _const_spec((d, 3 * d)), _const_spec((ch, d)), _const_spec((ch, d)),
                  _const_spec((1, ch)), _const_spec((ch, ch)), _const_spec((1, ch)),
                  _const_spec((ch, d)), _const_spec((d, d)), _const_spec((1, d)), _const_spec((1, d))],
        out_specs=tok(d),
        out_shape=jax.ShapeDtypeStruct((t_tok, d), F32),
        compiler_params=_cparams("arbitrary"),
        name="merge_branches",
    )(x2, shift, scale, gate, z_hy, z_cf, ys[0], ys[1], p, w_gate, w_hy, w_cf, d_skip, w_glu, b_glu,
      w_s5, w_o, ln_g, ln_b)


def _ffn_kernel(x_ref, sh_ref, sc_ref, gt_ref, w1_ref, w3_ref, w2_ref, lg_ref, lb_ref, o_ref, acc_ref,
                *, alpha, tk):
    x = x_ref[...]
    h = _ln_mod(x, sh_ref[0], sc_ref[0]).astype(BF16)
    for k in range(w1_ref.shape[1] // tk):
        cols = slice(k * tk, (k + 1) * tk)
        act = _silu(_bdot(h, w1_ref[:, cols])) * _bdot(h, w3_ref[:, cols])
        y = _bdot(act.astype(BF16), w2_ref[cols, :])
        if k == 0:
            acc_ref[...] = y
        else:
            acc_ref[...] += y
    o_ref[...] = _ln(alpha * x + gt_ref[0] * acc_ref[...]) * lg_ref[...] + lb_ref[...]


def ffn_dense(x2, shift, scale, gate, w1, w3, w2, ln_g, ln_b, *, seq, alpha, shared_mod=False, tk=256):
    t_tok, d = x2.shape
    dff = w1.shape[1]
    tm, brow = _token_tiling(t_tok, seq, 1024, shared_mod)
    mod_spec = pl.BlockSpec((1, 1, d), lambda i: (brow(i), 0, 0))
    return pl.pallas_call(
        functools.partial(_ffn_kernel, alpha=alpha, tk=tk),
        grid=(t_tok // tm,),
        in_specs=[pl.BlockSpec((tm, d), lambda i: (i, 0)), mod_spec, mod_spec, mod_spec,
                  _const_spec((d, dff)), _const_spec((d, dff)), _const_spec((dff, d)),
                  _const_spec((1, d)), _const_spec((1, d))],
        out_specs=pl.BlockSpec((tm, d), lambda i: (i, 0)),
        out_shape=jax.ShapeDtypeStruct((t_tok, d), F32),
        scratch_shapes=[pltpu.VMEM((tm, d), F32)],
        compiler_params=_cparams("arbitrary"),
        name="ffn_dense",
    )(x2, shift, scale, gate, w1, w3, w2, ln_g, ln_b)


def _router_top2(hf, router):
    logits = _dot3(_split_bf16(hf), _split_bf16(router))
    lane = lax.broadcasted_iota(jnp.int32, logits.shape, 1).astype(F32)
    neg = jnp.float32(-jnp.inf)
    logits = jnp.where(lane < N_EXPERTS, logits, neg)
    m1 = jnp.max(logits, axis=-1, keepdims=True)
    i1 = jnp.min(jnp.where(logits == m1, lane, float(ROUTER_LANES)), axis=-1, keepdims=True)
    rest = jnp.where(lane == i1, neg, logits)
    m2 = jnp.max(rest, axis=-1, keepdims=True)
    i2 = jnp.min(jnp.where(rest == m2, lane, float(ROUTER_LANES)), axis=-1, keepdims=True)
    e2 = jnp.exp(m2 - m1)
    return i1, i2, 1.0 / (1.0 + e2), e2 / (1.0 + e2)


def _route_kernel(x_ref, sh_ref, sc_ref, r_ref, o_ref):
    hf = _ln_mod(x_ref[...], sh_ref[0], sc_ref[0])
    i1, i2, w1, w2 = _router_top2(hf, r_ref[...])
    lane = lax.broadcasted_iota(jnp.int32, o_ref.shape, 1)
    o_ref[...] = jnp.where(lane == 0, i1, jnp.where(lane == 1, i2, jnp.where(lane == 2, w1,
                           jnp.where(lane == 3, w2, 0.0))))


def moe_route(x2, shift, scale, router, *, seq):
    t_tok, d = x2.shape
    tm = min(seq, 1024)
    nt = seq // tm
    mod_spec = pl.BlockSpec((1, 1, d), lambda i: (i // nt, 0, 0))
    return pl.pallas_call(
        _route_kernel,
        grid=(t_tok // tm,),
        in_specs=[pl.BlockSpec((tm, d), lambda i: (i, 0)), mod_spec, mod_spec,
                  pl.BlockSpec((d, ROUTER_LANES), lambda i: (0, 0))],
        out_specs=pl.BlockSpec((tm, ROUTE_COLS), lambda i: (i, 0)),
        out_shape=jax.ShapeDtypeStruct((t_tok, ROUTE_COLS), F32),
        compiler_params=_cparams("arbitrary"),
        name="moe_route",
    )(x2, shift, scale, router)


def _dispatch_kernel(pos_ref, x_ref, sh_ref, sc_ref, xs_in_ref, xs_ref, hbuf, sem, *, tm):
    del xs_in_ref
    i = pl.program_id(0)
    n = pl.num_programs(0)
    slot = i % 2

    def row_copy(s, r, p):
        return pltpu.make_async_copy(hbuf.at[s, pl.ds(r, 1)], xs_ref.at[pl.ds(p, 1)], sem.at[s])

    def wait_slot(s):
        for _ in range(TOP_K):
            pltpu.make_async_copy(hbuf.at[s], xs_ref.at[pl.ds(0, tm)], sem.at[s]).wait()

    @pl.when(i >= 2)
    def _():
        wait_slot(slot)

    hbuf[slot] = _ln_mod(x_ref[...], sh_ref[0], sc_ref[0])

    def body(r, carry):
        base = (i * tm + r) * TOP_K
        for k in range(TOP_K):
            row_copy(slot, r, pos_ref[base + k]).start()
        return carry

    lax.fori_loop(0, tm, body, 0, unroll=8)

    @pl.when(i == n - 1)
    def _():
        wait_slot(slot)

    @pl.when((i == n - 1) & (i >= 1))
    def _():
        wait_slot(1 - slot)


def moe_dispatch(pos, x2, shift, scale, n_rows, *, seq, tm=512):
    t_tok, d = x2.shape
    tm = min(tm, seq)
    nt = seq // tm
    mod_spec = pl.BlockSpec((1, 1, d), lambda i, p: (i // nt, 0, 0))
    return pl.pallas_call(
        functools.partial(_dispatch_kernel, tm=tm),
        grid_spec=pltpu.PrefetchScalarGridSpec(
            num_scalar_prefetch=1,
            grid=(t_tok // tm,),
            in_specs=[pl.BlockSpec((tm, d), lambda i, p: (i, 0)), mod_spec, mod_spec,
                      pl.BlockSpec(memory_space=pl.ANY)],
            out_specs=pl.BlockSpec(memory_space=pl.ANY),
            scratch_shapes=[pltpu.VMEM((2, tm, d), F32), pltpu.SemaphoreType.DMA((2,))]),
        out_shape=jax.ShapeDtypeStruct((n_rows, d), F32),
        input_output_aliases={4: 0},
        compiler_params=_cparams("arbitrary"),
        name="moe_dispatch",
    )(pos, x2, shift, scale, jnp.zeros((n_rows, d), F32))


def _expert_kernel(te_ref, na_ref, xs_ref, w1_ref, w3_ref, w2_ref, o_ref, *, tk):
    active = pl.program_id(0) < na_ref[0]

    @pl.when(active)
    def _():
        xb = xs_ref[...].astype(BF16)
        for k in range(w1_ref.shape[-1] // tk):
            cols = slice(k * tk, (k + 1) * tk)
            act = _silu(_bdot(xb, w1_ref[0, :, cols])) * _bdot(xb, w3_ref[0, :, cols])
            y = _bdot(act.astype(BF16), w2_ref[0, cols, :])
            if k == 0:
                o_ref[...] = y
            else:
                o_ref[...] += y

    @pl.when(jnp.logical_not(active))
    def _():
        o_ref[...] = jnp.zeros_like(o_ref)


def moe_experts(tile_expert, n_active, xs, w1, w3, w2, *, tg, tk=512):
    n_rows, d = xs.shape
    dff = w1.shape[-1]
    wspec = lambda shape: pl.BlockSpec(shape, lambda t, te, na: (te[t], 0, 0), pipeline_mode=pl.Buffered(1))
    return pl.pallas_call(
        functools.partial(_expert_kernel, tk=tk),
        grid_spec=pltpu.PrefetchScalarGridSpec(
            num_scalar_prefetch=2,
            grid=(n_rows // tg,),
            in_specs=[pl.BlockSpec((tg, d), lambda t, te, na: (t, 0)),
                      wspec((1, d, dff)), wspec((1, d, dff)), wspec((1, dff, d))],
            out_specs=pl.BlockSpec((tg, d), lambda t, te, na: (t, 0))),
        out_shape=jax.ShapeDtypeStruct((n_rows, d), F32),
        compiler_params=_cparams("arbitrary"),
        name="moe_experts",
    )(tile_expert, n_active, xs, w1, w3, w2)


def _combine_kernel(pos_ref, x_ref, gt_ref, rt_ref, lg_ref, lb_ref, ys_ref, o_ref, ybuf, sem, *, tm, alpha):
    i = pl.program_id(0)
    n = pl.num_programs(0)
    slot = i % 2

    def issue(step, s):
        def body(r, carry):
            base = (step * tm + r) * TOP_K
            for k in range(TOP_K):
                pltpu.make_async_copy(ys_ref.at[pl.ds(pos_ref[base + k], 1)], ybuf.at[s, k, pl.ds(r, 1)],
                                      sem.at[s]).start()
            return carry
        lax.fori_loop(0, tm, body, 0, unroll=8)

    @pl.when(i == 0)
    def _():
        issue(0, 0)

    @pl.when(i + 1 < n)
    def _():
        issue(i + 1, 1 - slot)

    for k in range(TOP_K):
        pltpu.make_async_copy(ys_ref.at[pl.ds(0, tm)], ybuf.at[slot, k], sem.at[slot]).wait()
    rt = rt_ref[...]
    y = rt[:, 2:3] * ybuf[slot, 0] + rt[:, 3:4] * ybuf[slot, 1]
    o_ref[...] = _ln(alpha * x_ref[...] + gt_ref[0] * y) * lg_ref[...] + lb_ref[...]


def moe_combine(pos, x2, gate, route, ln_g, ln_b, ys, *, seq, alpha, tm=512):
    t_tok, d = x2.shape
    tm = min(tm, seq)
    nt = seq // tm
    vec = pl.BlockSpec((1, d), lambda i, p: (0, 0))
    return pl.pallas_call(
        functools.partial(_combine_kernel, tm=tm, alpha=alpha),
        grid_spec=pltpu.PrefetchScalarGridSpec(
            num_scalar_prefetch=1,
            grid=(t_tok // tm,),
            in_specs=[pl.BlockSpec((tm, d), lambda i, p: (i, 0)),
                      pl.BlockSpec((1, 1, d), lambda i, p: (i // nt, 0, 0)),
                      pl.BlockSpec((tm, ROUTE_COLS), lambda i, p: (i, 0)), vec, vec,
                      pl.BlockSpec(memory_space=pl.ANY)],
            out_specs=pl.BlockSpec((tm, d), lambda i, p: (i, 0)),
            scratch_shapes=[pltpu.VMEM((2, TOP_K, tm, d), F32), pltpu.SemaphoreType.DMA((2,))]),
        out_shape=jax.ShapeDtypeStruct((t_tok, d), F32),
        compiler_params=_cparams("arbitrary"),
        name="moe_combine",
    )(pos, x2, gate, route, ln_g, ln_b, ys)


def _dispatch_plan(route, *, tg):
    ids = route[:, 0:TOP_K].astype(jnp.int32).reshape(-1)
    n_assign = ids.shape[0]
    onehot = (ids[:, None] == jnp.arange(N_EXPERTS, dtype=jnp.int32)[None, :]).astype(jnp.int32)
    csum = jnp.cumsum(onehot, axis=0)
    counts = csum[-1]
    padded = (counts + tg - 1) // tg * tg
    ends = jnp.cumsum(padded)
    starts = ends - padded
    pos = jnp.sum(onehot * (starts[None, :] + csum - 1), axis=1).astype(jnp.int32)
    n_tiles = n_assign // tg + N_EXPERTS
    tile_start = jnp.arange(n_tiles, dtype=jnp.int32) * tg
    tile_expert = jnp.minimum(jnp.sum(tile_start[:, None] >= ends[None, :], axis=1), N_EXPERTS - 1)
    n_active = (ends[-1] // tg).reshape(1)
    return pos, tile_expert.astype(jnp.int32), n_active.astype(jnp.int32), n_tiles * tg


def ffn_moe(x2, shift, scale, gate, router, w1, w3, w2, ln_g, ln_b, *, seq, alpha, tg=512):
    route = moe_route(x2, shift, scale, router, seq=seq)
    pos, tile_expert, n_active, n_rows = _dispatch_plan(route, tg=tg)
    xs = moe_dispatch(pos, x2, shift, scale, n_rows, seq=seq)
    ys = moe_experts(tile_expert, n_active, xs, w1, w3, w2, tg=tg)
    return moe_combine(pos, x2, gate, route, ln_g, ln_b, ys, seq=seq, alpha=alpha)


def _dft_matrices(rows, cols, n):
    f = lax.broadcasted_iota(jnp.int32, (rows, cols), 0)
    t = lax.broadcasted_iota(jnp.int32, (rows, cols), 1)
    ang = ((f * t) % n).astype(F32) * (2.0 * math.pi / n)
    return jnp.cos(ang), jnp.sin(ang)


def _twiddles(seq, width):
    ang = jnp.arange(seq // 2, dtype=F32) * (math.pi / seq)
    return jnp.broadcast_to(jnp.stack([jnp.cos(ang), jnp.sin(ang)])[:, :, None], (2, seq // 2, width))


def _position_features(seq, n_bands, width):
    t = jnp.arange(seq, dtype=F32)[:, None]
    tn = t / max(seq - 1, 1)
    bands = jnp.arange(1, n_bands + 1, dtype=F32)
    ang = t * bands * (2.0 * math.pi / seq)
    z = jnp.concatenate([tn, jnp.cos(ang), jnp.sin(ang)], axis=-1)
    return jnp.pad(z, ((0, 0), (0, width - z.shape[1])))


def _s5_discretize(a_re, a_im, log_dt, b_re, b_im):
    lam_re = jnp.minimum(a_re, S5_MAX_RE)
    lam_im = a_im
    dt = jnp.exp(log_dt)[..., None]
    mag = jnp.exp(lam_re * dt)
    ang = lam_im * dt
    abar_re = mag * jnp.cos(ang)
    abar_im = mag * jnp.sin(ang)
    den = lam_re * lam_re + lam_im * lam_im
    q_re = ((abar_re - 1.0) * lam_re + abar_im * lam_im) / den
    q_im = (abar_im * lam_re - (abar_re - 1.0) * lam_im) / den
    bb_re = q_re[..., None] * b_re - q_im[..., None] * b_im
    bb_im = q_re[..., None] * b_im + q_im[..., None] * b_re
    return abar_re, abar_im, bb_re, bb_im


def _s5_matrices(a_re, a_im, log_dt, b_re, b_im, c_re, c_im):
    abr, abi, bbr, bbi = _s5_discretize(a_re, a_im, log_dt, b_re, b_im)
    g_all, p, k = bbr.shape[1:]
    gb = S5_BLOCK_CH // S5_GROUP
    nj = g_all // gb
    eye = jnp.eye(gb, dtype=F32)

    def in_mat(bb):
        t = jnp.swapaxes(bb, -1, -2).reshape(2, nj, gb, k, p)
        return jnp.einsum("djgkp,gh->djgkhp", t, eye).reshape(2, nj, gb * k, gb * p)

    def out_mat(cc):
        t = jnp.swapaxes(cc, -1, -2).reshape(2, nj, gb, p, k)
        return jnp.einsum("djhpc,hg->djhpgc", t, eye).reshape(2, nj, gb * p, gb * k)

    bmat = jnp.concatenate([in_mat(bbr), in_mat(bbi)], axis=-1).astype(BF16)
    cmat = jnp.concatenate([out_mat(c_re), -out_mat(c_im)], axis=-2).astype(BF16)
    abar = jnp.stack([abr.reshape(2, nj, gb * p), abi.reshape(2, nj, gb * p)], axis=2)
    return bmat, cmat, abar


def kernel(x, c, ctx, c_ctx, w_mod, b_mod, w_in, hy_short_w, hy_short_b, hy_f_w1, hy_f_b1, hy_f_w2, hy_f_b2, hy_f_w3, hy_f_b3, hy_freq, hy_decay, hy_bias, w_hy_out, cf_dw_w, cf_dw_b, cf_ln_g, cf_ln_b, w_cf_out, s5_a_re, s5_a_im, s5_log_dt, s5_b_re, s5_b_im, s5_c_re, s5_c_im, s5_d, s5_w_glu, s5_b_glu, w_s5_out, w_o, ln1_g, ln1_b, ln2_g, ln2_b, ffn_w1, ffn_w3, ffn_w2, moe_router, moe_w1, moe_w3, moe_w2):
    bsz, seq, d = x.shape
    seq_c = ctx.shape[1]
    depth = w_mod.shape[0]
    alpha = (2.0 * depth) ** 0.25
    ch = D_BRANCH
    hy_cols = (HY_ORDER + 1) * ch
    cf_off = hy_cols
    s5_off = cf_off + 2 * ch
    gate_off = s5_off + ch
    n_bands = (hy_f_w1.shape[1] - 1) // 2
    nj = ch // S5_BLOCK_CH

    x2 = x.reshape(bsz * seq, d)
    xc2 = ctx.reshape(bsz * seq_c, d)
    mod_rows = 24
    cc = jnp.zeros((mod_rows, d), F32).at[:bsz].set(c).at[bsz].set(c_ctx)

    dft = {}
    feats = {}
    for s in (seq, seq_c):
        cs, ss = _dft_matrices(s // 2, s, 2 * s)
        csh, ssh = _dft_matrices(s // 2, s // 2, s)
        dft[s] = (cs, ss, csh.astype(BF16), ssh.astype(BF16), _twiddles(s, ch))
        feats[s] = _position_features(s, n_bands, 128)
    zero_state = jnp.zeros((2, nj, bsz, 2 * S5_BLOCK_ST), F32)
    row = lambda v: v[None, :]

    for l in range(depth):
        last = l == depth - 1
        mods = mod_vectors(cc, w_mod[l].astype(BF16), row(b_mod[l])).reshape(mod_rows, 6, d)
        lat = [mods[:bsz, i][:, None, :] for i in range(6)]
        cxm = [jnp.broadcast_to(mods[bsz, i][None, None, :], (bsz, 1, d)) for i in range(6)]

        w_in_b = w_in[l].astype(BF16)
        w_mix = w_in_b[:, :gate_off]
        w_s5 = w_in_b[:, s5_off:gate_off]
        w_gate = w_in_b[:, gate_off:]
        sw = jnp.concatenate([hy_short_w[l][:, i * ch:(i + 1) * ch] for i in range(HY_ORDER + 1)], axis=0)
        sb = hy_short_b[l].reshape(HY_ORDER + 1, ch)
        w1p = jnp.pad(hy_f_w1[l], ((0, 128 - hy_f_w1.shape[1]), (0, 0)))
        bmat, cmat, abar = _s5_matrices(s5_a_re[l], s5_a_im[l], s5_log_dt[l], s5_b_re[l], s5_b_im[l],
                                        s5_c_re[l], s5_c_im[l])
        merge_w = (w_gate, w_hy_out[l].astype(BF16), w_cf_out[l].astype(BF16), row(s5_d[l]),
                   s5_w_glu[l].astype(BF16), row(s5_b_glu[l]), w_s5_out[l].astype(BF16),
                   w_o[l].astype(BF16), row(ln1_g[l]), row(ln1_b[l]))

        def spectra(s):
            filt = hyena_filters(feats[s], w1p, row(hy_f_b1[l]), hy_f_w2[l], row(hy_f_b2[l]), hy_f_w3[l],
                                 row(hy_f_b3[l]), hy_freq[l], row(hy_decay[l]))
            return hyena_spectra(dft[s][0], dft[s][1], filt)

        def token_mixer(xs, s, seg, m, init, shared):
            sh1, sc1, g1 = m[0], m[1], m[2]
            p = ln_mod_proj(xs, sh1, sc1, w_mix, seq=s, shared_mod=shared)
            z_hy = hyena_mix(p, sw, sb, dft[s][2], dft[s][3], dft[s][4], *spectra(s), hy_bias[l], seq=s)
            z_cf = conformer_mix(p, cf_dw_w[l], row(cf_dw_b[l]), row(cf_ln_g[l]), row(cf_ln_b[l]), seg=seg,
                                 col_block=cf_off // ch)
            ys, fin = s5_scan(p, bmat, cmat, abar, init, seq=s, col_off=s5_off)
            x1 = merge_branches(xs, sh1, sc1, g1, z_hy, z_cf, ys, p, *merge_w, seq=s, alpha=alpha,
                                s5_col_block=s5_off // ch, shared_mod=shared)
            return x1, fin

        def channel_mixer(xs, s, m, shared):
            sh2, sc2, g2 = m[3], m[4], m[5]
            i = l // 2
            if l % 2 == 0:
                return ffn_dense(xs, sh2, sc2, g2, ffn_w1[i].astype(BF16), ffn_w3[i].astype(BF16),
                                 ffn_w2[i].astype(BF16), row(ln2_g[l]), row(ln2_b[l]), seq=s, alpha=alpha,
                                 shared_mod=shared)
            router = jnp.pad(moe_router[i], ((0, 0), (0, ROUTER_LANES - moe_router.shape[-1])))
            return ffn_moe(xs, sh2, sc2, g2, router, moe_w1[i].astype(BF16), moe_w3[i].astype(BF16),
                           moe_w2[i].astype(BF16), row(ln2_g[l]), row(ln2_b[l]), seq=s, alpha=alpha)

        if last:
            u_s5 = ln_mod_proj(xc2, cxm[0], cxm[1], w_s5, seq=seq_c, shared_mod=True)
            _, finals = s5_scan(u_s5, bmat, cmat, abar, zero_state, seq=seq_c, col_off=0)
        else:
            xc2, finals = token_mixer(xc2, seq_c, seq_c, cxm, zero_state, True)
            xc2 = channel_mixer(xc2, seq_c, cxm, True)

        x2, _ = token_mixer(x2, seq, GRID_W, lat, finals, False)
        x2 = channel_mixer(x2, seq, lat, False)
    return x2.reshape(bsz, seq, d)
```
